```python
import jax, jax.numpy as jnp
from jax import lax
import numpy as np

D_MODEL = 2048
BATCH = 4
SEQ = 8192
DEPTH = 2

GRID_W = 64
HEAD_DIM = 128
D_MIX = D_MODEL
FOURIER_GROUPS = D_MIX // 4 // HEAD_DIM
FOURIER_WIDTH = FOURIER_GROUPS * HEAD_DIM
MEM_HEADS = 4
MEM_WIDTH = MEM_HEADS * HEAD_DIM
NA_WIDTH = D_MIX - FOURIER_WIDTH - MEM_WIDTH
NA_HEADS = NA_WIDTH // HEAD_DIM
NA_KH = 8
NA_KW = 16
N_MEM = 256
D_IN = 3 * NA_WIDTH + FOURIER_WIDTH + MEM_WIDTH
D_FF = ((8 * D_MODEL // 3 + 255) // 256) * 256
EPS = 1e-6

kernel_name = "hybrid_natten_fnet_memory_encoder"


def rms_norm(x, g):
    xf = x.astype(jnp.float32)
    y = xf * lax.rsqrt(jnp.mean(xf * xf, axis=-1, keepdims=True) + EPS)
    return (y * g.astype(jnp.float32)).astype(x.dtype)


def neighbourhood_attention(q, k, v, rpb):
    B, S, H, Dh = q.shape
    rows = S // GRID_W
    kh = min(NA_KH, rows)
    kw = min(NA_KW, GRID_W)
    scale = Dh ** -0.5
    qg = q.reshape(B, rows, GRID_W, H, Dh)
    kg = k.reshape(B, rows, GRID_W, H, Dh)
    vg = v.reshape(B, rows, GRID_W, H, Dh)

    col = jnp.arange(GRID_W)
    col_start = jnp.clip(col - kw // 2, 0, GRID_W - kw)
    col_mask = (col[None, :] >= col_start[:, None]) & (col[None, :] < col_start[:, None] + kw)
    col_idx = jnp.clip(col[None, :] - col[:, None] + NA_KW - 1, 0, 2 * NA_KW - 2)
    bias_col = rpb[:, :, col_idx]

    def one_row(r):
        start = jnp.clip(r - kh // 2, 0, rows - kh)
        q_r = lax.dynamic_index_in_dim(qg, r, axis=1, keepdims=False)
        k_b = lax.dynamic_slice_in_dim(kg, start, kh, axis=1)
        v_b = lax.dynamic_slice_in_dim(vg, start, kh, axis=1)
        s = jnp.einsum('bqhd,bakhd->bhqak', q_r, k_b,
                       preferred_element_type=jnp.float32) * scale
        row_idx = start + jnp.arange(kh) - r + NA_KH - 1
        bias = jnp.take(bias_col, row_idx, axis=1).transpose(0, 2, 1, 3)
        s = s + bias[None].astype(jnp.float32)
        s = jnp.where(col_mask[None, None, :, None, :], s, -jnp.inf)
        p = jax.nn.softmax(s.reshape(B, H, GRID_W, kh * GRID_W), axis=-1)
        p = p.reshape(B, H, GRID_W, kh, GRID_W).astype(v.dtype)
        return jnp.einsum('bhqak,bakhd->bqhd', p, v_b)

    out = lax.map(one_row, jnp.arange(rows))
    return out.transpose(1, 0, 2, 3, 4).reshape(B, S, H * Dh)


def fourier_mix(u, w_f):
    B, S, _ = u.shape
    uf = u.astype(jnp.float32).reshape(B, S, FOURIER_GROUPS, HEAD_DIM)
    y = jnp.fft.fft2(uf, axes=(1, 3), norm="ortho").real
    y = jnp.einsum('bsgc,gce->bsge', y, w_f.astype(jnp.float32))
    return y.reshape(B, S, FOURIER_WIDTH).astype(u.dtype)


def memory_attention(q, mk, mv):
    B, S, H, Dh = q.shape
    s = jnp.einsum('bshd,bmhd->bhsm', q, mk, preferred_element_type=jnp.float32) * (Dh ** -0.5)
    p = jax.nn.softmax(s, axis=-1).astype(mv.dtype)
    return jnp.einsum('bhsm,bmhd->bshd', p, mv).reshape(B, S, H * Dh)


def setup_inputs(seed: int = 0) -> dict:
    key = jax.random.key(seed)
    ks = jax.random.split(key, 20)
    f32 = jnp.float32

    def nrm(k, shape, scale):
        return jax.random.normal(k, shape, f32) * scale

    def gain(k, shape):
        return 1.0 + 0.05 * jax.random.normal(k, shape, f32)

    return {
        "x": jax.random.normal(ks[0], (BATCH, SEQ, D_MODEL), f32),
        "mem": jax.random.normal(ks[1], (BATCH, N_MEM, D_MODEL), f32),
        "attn_norm": gain(ks[2], (DEPTH, D_MODEL)),
        "w_in": nrm(ks[3], (DEPTH, D_MODEL, D_IN), D_MODEL ** -0.5),
        "na_q_norm": gain(ks[4], (DEPTH, HEAD_DIM)),
        "na_k_norm": gain(ks[5], (DEPTH, HEAD_DIM)),
        "na_rpb": nrm(ks[6], (DEPTH, NA_HEADS, 2 * NA_KH - 1, 2 * NA_KW - 1), 0.5),
        "w_fourier": nrm(ks[7], (DEPTH, FOURIER_GROUPS, HEAD_DIM, HEAD_DIM), HEAD_DIM ** -0.5),
        "mem_norm": gain(ks[8], (DEPTH, D_MODEL)),
        "w_mem_kv": nrm(ks[9], (DEPTH, D_MODEL, 2 * MEM_WIDTH), D_MODEL ** -0.5),
        "mem_q_norm": gain(ks[10], (DEPTH, HEAD_DIM)),
        "mem_k_norm": gain(ks[11], (DEPTH, HEAD_DIM)),
        "out_norm": gain(ks[12], (DEPTH, D_MIX)),
        "w_out": nrm(ks[13], (DEPTH, D_MIX, D_MODEL), D_MIX ** -0.5 * 0.5),
        "ffn_norm": gain(ks[14], (DEPTH, D_MODEL)),
        "w_gate": nrm(ks[15], (DEPTH, D_MODEL, D_FF), D_MODEL ** -0.5),
        "w_up": nrm(ks[16], (DEPTH, D_MODEL, D_FF), D_MODEL ** -0.5),
        "w_down": nrm(ks[17], (DEPTH, D_FF, D_MODEL), D_FF ** -0.5 * 0.5),
    }


def reference(x, mem, attn_norm, w_in, na_q_norm, na_k_norm, na_rpb, w_fourier,
              mem_norm, w_mem_kv, mem_q_norm, mem_k_norm, out_norm, w_out,
              ffn_norm, w_gate, w_up, w_down):
    B, S, _ = x.shape
    M = mem.shape[1]
    for l in range(DEPTH):
        h = rms_norm(x, attn_norm[l])
        proj = h @ w_in[l]
        o1 = NA_WIDTH
        o2 = 2 * NA_WIDTH
        o3 = 3 * NA_WIDTH
        o4 = o3 + FOURIER_WIDTH
        q_na = rms_norm(proj[..., :o1].reshape(B, S, NA_HEADS, HEAD_DIM), na_q_norm[l])
        k_na = rms_norm(proj[..., o1:o2].reshape(B, S, NA_HEADS, HEAD_DIM), na_k_norm[l])
        v_na = proj[..., o2:o3].reshape(B, S, NA_HEADS, HEAD_DIM)
        u_f = proj[..., o3:o4]
        q_m = rms_norm(proj[..., o4:].reshape(B, S, MEM_HEADS, HEAD_DIM), mem_q_norm[l])

        mem_n = rms_norm(mem, mem_norm[l])
        kv_m = mem_n @ w_mem_kv[l]
        k_m = rms_norm(kv_m[..., :MEM_WIDTH].reshape(B, M, MEM_HEADS, HEAD_DIM), mem_k_norm[l])
        v_m = kv_m[..., MEM_WIDTH:].reshape(B, M, MEM_HEADS, HEAD_DIM)

        y_na = neighbourhood_attention(q_na, k_na, v_na, na_rpb[l])
        y_f = fourier_mix(u_f, w_fourier[l])
        y_m = memory_attention(q_m, k_m, v_m)

        y = jnp.concatenate([y_na, y_f, y_m], axis=-1)
        y = rms_norm(y.reshape(B, S, D_MIX // HEAD_DIM, HEAD_DIM),
                     jnp.ones((HEAD_DIM,), x.dtype)).reshape(B, S, D_MIX) * out_norm[l]
        x = x + y @ w_out[l]

        h = rms_norm(x, ffn_norm[l])
        x = x + (jax.nn.silu(h @ w_gate[l]) * (h @ w_up[l])) @ w_down[l]
    return x
```

```python
import functools

import numpy as np
import jax
import jax.numpy as jnp
from jax import lax
from jax.experimental import pallas as pl
from jax.experimental.pallas import tpu as pltpu

F32 = jnp.float32
BF16 = jnp.bfloat16

EPS = 1e-6
HEAD_DIM = 128
GRID_W = 64
NA_KH = 8
NA_KW = 16
NA_ROWS_PER_BLOCK = 8
NA_HALO_ROWS = 4
MASK_VALUE = -1e30

DFT_R1 = 64
DFT_SUB = 8

MIB = 1024 * 1024
VMEM_LIMIT = 48 * MIB


def _cparams(sem, vmem=VMEM_LIMIT):
    return pltpu.CompilerParams(dimension_semantics=sem, vmem_limit_bytes=vmem)


def _rms(x, eps=EPS):
    return x * lax.rsqrt(jnp.mean(x * x, axis=-1, keepdims=True) + eps)


def _head_norm_store(dst_ref, acc, gain, n_heads, col0=0):
    for h in range(n_heads):
        sl = slice(col0 + h * HEAD_DIM, col0 + (h + 1) * HEAD_DIM)
        dl = slice(h * HEAD_DIM, (h + 1) * HEAD_DIM)
        dst_ref[:, dl] = (_rms(acc[:, sl]) * gain[:, sl]).astype(dst_ref.dtype)


def _inproj_kernel(x_ref, g_ref, w_ref, hg_ref, q_ref, k_ref, v_ref, u_ref, qm_ref, h_scr):
    j = pl.program_id(1)

    @pl.when(j == 0)
    def _():
        h_scr[...] = (_rms(x_ref[...]) * g_ref[...]).astype(BF16)

    acc = jnp.dot(h_scr[...], w_ref[...], preferred_element_type=F32)
    hg = hg_ref[...]
    n_heads = acc.shape[1] // HEAD_DIM

    @pl.when(j == 0)
    def _():
        _head_norm_store(q_ref, acc, hg, n_heads)

    @pl.when(j == 1)
    def _():
        _head_norm_store(k_ref, acc, hg, n_heads)

    @pl.when(j == 2)
    def _():
        v_ref[...] = acc.astype(v_ref.dtype)

    @pl.when(j == 3)
    def _():
        half = acc.shape[1] // 2
        u_ref[...] = acc[:, :half]
        _head_norm_store(qm_ref, acc, hg, n_heads // 2, col0=half)


def _in_projection(x2, gain, w_in, head_gain, tm=1024):
    n, d = x2.shape
    d_in = w_in.shape[1]
    tn = d_in // 4
    half = tn // 2
    grid = (n // tm, 4)
    row = lambda i, j: (i, 0)
    return pl.pallas_call(
        _inproj_kernel,
        grid=grid,
        in_specs=[
            pl.BlockSpec((tm, d), row),
            pl.BlockSpec((1, d), lambda i, j: (0, 0)),
            pl.BlockSpec((d, tn), lambda i, j: (0, j)),
            pl.BlockSpec((1, tn), lambda i, j: (0, j)),
        ],
        out_specs=[
            pl.BlockSpec((tm, tn), row),
            pl.BlockSpec((tm, tn), row),
            pl.BlockSpec((tm, tn), row),
            pl.BlockSpec((tm, half), row),
            pl.BlockSpec((tm, half), row),
        ],
        out_shape=[
            jax.ShapeDtypeStruct((n, tn), BF16),
            jax.ShapeDtypeStruct((n, tn), BF16),
            jax.ShapeDtypeStruct((n, tn), BF16),
            jax.ShapeDtypeStruct((n, half), F32),
            jax.ShapeDtypeStruct((n, half), BF16),
        ],
        scratch_shapes=[pltpu.VMEM((tm, d), BF16)],
        compiler_params=_cparams(("parallel", "arbitrary"), vmem=56 * MIB),
        name="in_projection",
    )(x2, gain, w_in, head_gain)


def _na_kernel(q_ref, kp_ref, kc_ref, kn_ref, vp_ref, vc_ref, vn_ref, bias_ref, g_ref, o_ref):
    n_batch = q_ref.shape[0]

    def body(b, carry):
        q = q_ref[b]
        k = jnp.concatenate([kp_ref[b], kc_ref[b], kn_ref[b]], axis=0)
        v = jnp.concatenate([vp_ref[b], vc_ref[b], vn_ref[b]], axis=0)
        s = lax.dot_general(q, k, (((1,), (1,)), ((), ())), preferred_element_type=F32)
        s = s + bias_ref[0, 0]
        m = jnp.max(s, axis=-1, keepdims=True)
        p = jnp.exp(s - m)
        l = jnp.sum(p, axis=-1, keepdims=True)
        o = jnp.dot(p.astype(BF16), v, preferred_element_type=F32) / l
        o_ref[b] = (_rms(o) * g_ref[...]).astype(o_ref.dtype)
        return carry

    lax.fori_loop(0, n_batch, body, 0)


def _na_bias_table(rpb, rows):
    n_heads = rpb.shape[0]
    rq, halo = NA_ROWS_PER_BLOCK, NA_HALO_ROWS
    rk = rq + 2 * halo
    kh = min(NA_KH, rows)
    kw = min(NA_KW, GRID_W)
    col = np.arange(GRID_W)
    col_start = np.clip(col - kw // 2, 0, GRID_W - kw)
    col_ok = (col[None, :] >= col_start[:, None]) & (col[None, :] < col_start[:, None] + kw)
    col_idx = np.clip(col[None, :] - col[:, None] + NA_KW - 1, 0, 2 * NA_KW - 2)
    toe = jnp.where(col_ok[None, None], rpb[:, :, col_idx], MASK_VALUE)
    toe = jnp.concatenate([toe, jnp.full((n_heads, 1, GRID_W, GRID_W), MASK_VALUE, F32)], axis=1)
    invalid = 2 * NA_KH - 1
    n_blocks = rows // rq
    sel = np.full((3, rq, rk), invalid, np.int32)
    for vi, rb in enumerate((0, 1, n_blocks - 1)):
        for il in range(rq):
            r = rq * rb + il
            start = min(max(r - kh // 2, 0), rows - kh)
            for jl in range(rk):
                key_row = rq * rb - halo + jl
                if start <= key_row < start + kh:
                    sel[vi, il, jl] = key_row - r + NA_KH - 1
    tab = jnp.take(toe, jnp.asarray(sel.reshape(-1)), axis=1)
    tab = tab.reshape(n_heads, 3, rq, rk, GRID_W, GRID_W)
    tab = tab.transpose(1, 0, 2, 4, 3, 5)
    return tab.reshape(3, n_heads, rq * GRID_W, rk * GRID_W)


def _neighbourhood_attention(q, k, v, bias, gain):
    b, s, width = q.shape
    n_heads = width // HEAD_DIM
    tq = NA_ROWS_PER_BLOCK * GRID_W
    th = NA_HALO_ROWS * GRID_W
    n_blocks = s // tq
    n_halo_blocks = s // th
    per = tq // th

    cur = lambda h, rb: (0, rb, h)
    prev = lambda h, rb: (0, jnp.maximum(per * rb - 1, 0), h)
    nxt = lambda h, rb: (0, jnp.minimum(per * rb + per, n_halo_blocks - 1), h)
    variant = lambda h, rb: ((rb > 0).astype(jnp.int32) + (rb == n_blocks - 1).astype(jnp.int32), h, 0, 0)

    blk_c = pl.BlockSpec((b, tq, HEAD_DIM), cur)
    blk_p = pl.BlockSpec((b, th, HEAD_DIM), prev)
    blk_n = pl.BlockSpec((b, th, HEAD_DIM), nxt)
    return pl.pallas_call(
        _na_kernel,
        grid=(n_heads, n_blocks),
        in_specs=[
            blk_c, blk_p, blk_c, blk_n, blk_p, blk_c, blk_n,
            pl.BlockSpec((1, 1, tq, tq + 2 * th), variant),
            pl.BlockSpec((1, HEAD_DIM), lambda h, rb: (0, h)),
        ],
        out_specs=blk_c,
        out_shape=jax.ShapeDtypeStruct((b, s, width), BF16),
        compiler_params=_cparams(("parallel", "arbitrary")),
        name="neighbourhood_attention",
    )(q, k, k, k, v, v, v, bias, gain)


def _mem_kv_kernel(m_ref, g_ref, w_ref, kg_ref, k_ref, v_ref, h_scr):
    j = pl.program_id(0)

    @pl.when(j == 0)
    def _():
        h_scr[...] = (_rms(m_ref[...]) * g_ref[...]).astype(BF16)

    acc = jnp.dot(h_scr[...], w_ref[...], preferred_element_type=F32)

    @pl.when(j == 0)
    def _():
        _head_norm_store(k_ref, acc, kg_ref[...], acc.shape[1] // HEAD_DIM)

    @pl.when(j == 1)
    def _():
        v_ref[...] = acc.astype(v_ref.dtype)


def _memory_kv(mem2, gain, w_kv, k_gain):
    n, d = mem2.shape
    width = w_kv.shape[1] // 2
    full = lambda j: (0, 0)
    return pl.pallas_call(
        _mem_kv_kernel,
        grid=(2,),
        in_specs=[
            pl.BlockSpec((n, d), full),
            pl.BlockSpec((1, d), full),
            pl.BlockSpec((d, width), lambda j: (0, j)),
            pl.BlockSpec((1, width), full),
        ],
        out_specs=[pl.BlockSpec((n, width), full), pl.BlockSpec((n, width), full)],
        out_shape=[jax.ShapeDtypeStruct((n, width), BF16), jax.ShapeDtypeStruct((n, width), BF16)],
        scratch_shapes=[pltpu.VMEM((n, d), BF16)],
        compiler_params=_cparams(("arbitrary",)),
        name="memory_kv",
    )(mem2, gain, w_kv, k_gain)


def _mem_attn_kernel(q_ref, k_ref, v_ref, g_ref, o_ref):
    n_heads = q_ref.shape[2] // HEAD_DIM
    for h in range(n_heads):
        sl = slice(h * HEAD_DIM, (h + 1) * HEAD_DIM)
        s = lax.dot_general(q_ref[0, :, sl], k_ref[0, :, sl], (((1,), (1,)), ((), ())),
                            preferred_element_type=F32)
        m = jnp.max(s, axis=-1, keepdims=True)
        p = jnp.exp(s - m)
        l = jnp.sum(p, axis=-1, keepdims=True)
        o = jnp.dot(p.astype(BF16), v_ref[0, :, sl], preferred_element_type=F32) / l
        o_ref[0, :, sl] = (_rms(o) * g_ref[:, sl]).astype(o_ref.dtype)


def _memory_attention(q, k, v, gain, tm=1024):
    b, s, width = q.shape
    m = k.shape[1]
    return pl.pallas_call(
        _mem_attn_kernel,
        grid=(b, s // tm),
        in_specs=[
            pl.BlockSpec((1, tm, width), lambda bi, t: (bi, t, 0)),
            pl.BlockSpec((1, m, width), lambda bi, t: (bi, 0, 0)),
            pl.BlockSpec((1, m, width), lambda bi, t: (bi, 0, 0)),
            pl.BlockSpec((1, width), lambda bi, t: (0, 0)),
        ],
        out_specs=pl.BlockSpec((1, tm, width), lambda bi, t: (bi, t, 0)),
        out_shape=jax.ShapeDtypeStruct((b, s, width), BF16),
        compiler_params=_cparams(("parallel", "arbitrary")),
        name="memory_attention",
    )(q, k, v, gain)


def _dft_matrices(seq):
    r1 = DFT_R1
    r2 = seq // r1
    sub = DFT_SUB
    k2 = np.arange(r2, dtype=np.int64)
    s2 = np.arange(r2, dtype=np.int64)
    s1 = np.arange(r1, dtype=np.int64)
    k1 = np.arange(r1, dtype=np.int64)
    num = (k2[None, :, None] * s2[None, None, :] * r1 + s1[:, None, None] * k2[None, :, None]) % seq
    th = 2.0 * np.pi * num.astype(np.float64) / seq
    part = jnp.asarray(np.stack([np.cos(th), -np.sin(th)], axis=1), F32)
    part = part.reshape(r1 // sub, sub, 2, r2, r2)
    eye = jnp.asarray(np.eye(sub), F32)
    ma = part.transpose(0, 2, 3, 1, 4)[..., None] * eye[None, None, None, :, None, :]
    ma = ma.reshape(r1 // sub, 2 * r2 * sub, r2 * sub).astype(BF16)
    ph = 2.0 * np.pi * ((k1[:, None] * s1[None, :]) % r1).astype(np.float64) / r1
    c, s = np.cos(ph), np.sin(ph)
    g = np.stack([np.stack([c, s], 0), np.stack([s, -c], 0)], 0)
    mc = np.einsum("ocks,ab->okacbs", g, np.eye(sub))
    mc = jnp.asarray(mc.reshape(2 * r1 * sub, 2 * sub * r1), BF16)
    cc = np.arange(HEAD_DIM, dtype=np.int64)
    pc = 2.0 * np.pi * ((cc[:, None] * cc[None, :]) % HEAD_DIM).astype(np.float64) / HEAD_DIM
    mch = np.concatenate([np.cos(pc), -np.sin(pc)], axis=0) / np.sqrt(float(seq) * HEAD_DIM)
    return ma, mc, jnp.asarray(mch, BF16)


def _fft_a_kernel(u_ref, ma_ref, t_ref):
    r2, sub, width = u_ref.shape[1], u_ref.shape[2], u_ref.shape[3]
    xb = u_ref[0].reshape(r2 * sub, width).astype(BF16)
    res = jnp.dot(ma_ref[0], xb, preferred_element_type=F32)
    t_ref[0] = res.reshape(2, r2, sub, width)


def _fft_stage_a(u4, ma):
    b, r2, r1, width = u4.shape
    sub = DFT_SUB
    return pl.pallas_call(
        _fft_a_kernel,
        grid=(r1 // sub, b),
        in_specs=[
            pl.BlockSpec((1, r2, sub, width), lambda t, bi: (bi, 0, t, 0)),
            pl.BlockSpec((1, 2 * r2 * sub, r2 * sub), lambda t, bi: (t, 0, 0)),
        ],
        out_specs=pl.BlockSpec((1, 2, r2, sub, width), lambda t, bi: (bi, 0, 0, t, 0)),
        out_shape=jax.ShapeDtypeStruct((b, 2, r2, r1, width), F32),
        compiler_params=_cparams(("arbitrary", "arbitrary")),
        name="fourier_stage_a",
    )(u4, ma)


def _fft_c_kernel(t_ref, mc_ref, mch_ref, wf_ref, g_ref, o_ref):
    sub, r1, width = t_ref.shape[2], t_ref.shape[3], t_ref.shape[4]
    tb = t_ref[0].reshape(2 * sub * r1, width).astype(BF16)
    res = jnp.dot(mc_ref[...], tb, preferred_element_type=F32)
    half = r1 * sub
    for g in range(width // HEAD_DIM):
        sl = slice(g * HEAD_DIM, (g + 1) * HEAD_DIM)
        pq = jnp.concatenate([res[:half, sl], res[half:, sl]], axis=-1).astype(BF16)
        y = jnp.dot(pq, mch_ref[...], preferred_element_type=F32)
        z = jnp.dot(y.astype(BF16), wf_ref[g], preferred_element_type=F32)
        z = _rms(z) * g_ref[:, sl]
        o_ref[0, :, :, sl] = z.reshape(r1, sub, HEAD_DIM)


def _fft_stage_c(t5, mc, mch, w_f, gain):
    b, _, r2, r1, width = t5.shape
    sub = DFT_SUB
    groups = width // HEAD_DIM
    return pl.pallas_call(
        _fft_c_kernel,
        grid=(r2 // sub, b),
        in_specs=[
            pl.BlockSpec((1, 2, sub, r1, width), lambda t, bi: (bi, 0, t, 0, 0)),
            pl.BlockSpec(mc.shape, lambda t, bi: (0, 0)),
            pl.BlockSpec(mch.shape, lambda t, bi: (0, 0)),
            pl.BlockSpec((groups, HEAD_DIM, HEAD_DIM), lambda t, bi: (0, 0, 0)),
            pl.BlockSpec((1, width), lambda t, bi: (0, 0)),
        ],
        out_specs=pl.BlockSpec((1, r1, sub, width), lambda t, bi: (bi, 0, t, 0)),
        out_shape=jax.ShapeDtypeStruct((b, r1, r2, width), F32),
        compiler_params=_cparams(("arbitrary", "arbitrary")),
        name="fourier_stage_c",
    )(t5, mc, mch, w_f, gain)


def _outproj_kernel(x_ref, ya_ref, yf_ref, ym_ref, w_ref, o_ref):
    na = ya_ref.shape[1]
    nf = yf_ref.shape[1]
    acc = x_ref[...]
    acc = acc + jnp.dot(ya_ref[...], w_ref[:na, :], preferred_element_type=F32)
    acc = acc + jnp.dot(yf_ref[...].astype(BF16), w_ref[na:na + nf, :], preferred_element_type=F32)
    acc = acc + jnp.dot(ym_ref[...], w_ref[na + nf:, :], preferred_element_type=F32)
    o_ref[...] = acc


def _out_projection(x2, y_na, y_f, y_m, w_out, tm=1024, tn=1024):
    n, d = x2.shape
    row = lambda i, j: (i, 0)
    return pl.pallas_call(
        _outproj_kernel,
        grid=(n // tm, d // tn),
        in_specs=[
            pl.BlockSpec((tm, tn), lambda i, j: (i, j)),
            pl.BlockSpec((tm, y_na.shape[1]), row),
            pl.BlockSpec((tm, y_f.shape[1]), row),
            pl.BlockSpec((tm, y_m.shape[1]), row),
            pl.BlockSpec((w_out.shape[0], tn), lambda i, j: (0, j)),
        ],
        out_specs=pl.BlockSpec((tm, tn), lambda i, j: (i, j)),
        out_shape=jax.ShapeDtypeStruct((n, d), F32),
        compiler_params=_cparams(("parallel", "arbitrary")),
        name="out_projection",
    )(x2, y_na, y_f, y_m, w_out)


def _ffn_kernel(x_ref, g_ref, wg_ref, wu_ref, wd_ref, o_ref, h_scr):
    f = pl.program_id(1)

    @pl.when(f == 0)
    def _():
        x = x_ref[...]
        h_scr[...] = (_rms(x) * g_ref[...]).astype(BF16)
        o_ref[...] = x

    h = h_scr[...]
    gate = jnp.dot(h, wg_ref[...], preferred_element_type=F32)
    up = jnp.dot(h, wu_ref[...], preferred_element_type=F32)
    act = (gate * jax.nn.sigmoid(gate) * up).astype(BF16)
    o_ref[...] += jnp.dot(act, wd_ref[...], preferred_element_type=F32)


def _ffn(x2, gain, w_gate, w_up, w_down, tm=512, tf=512):
    n, d = x2.shape
    d_ff = w_gate.shape[1]
    row = lambda i, f: (i, 0)
    return pl.pallas_call(
        _ffn_kernel,
        grid=(n // tm, d_ff // tf),
        in_specs=[
            pl.BlockSpec((tm, d), row),
            pl.BlockSpec((1, d), lambda i, f: (0, 0)),
            pl.BlockSpec((d, tf), lambda i, f: (0, f)),
            pl.BlockSpec((d, tf), lambda i, f: (0, f)),
            pl.BlockSpec((tf, d), lambda i, f: (f, 0)),
        ],
        out_specs=pl.BlockSpec((tm, d), row),
        out_shape=jax.ShapeDtypeStruct((n, d), F32),
        scratch_shapes=[pltpu.VMEM((tm, d), BF16)],
        compiler_params=_cparams(("parallel", "arbitrary")),
        name="swiglu_ffn",
    )(x2, gain, w_gate, w_up, w_down)


def kernel(x, mem, attn_norm, w_in, na_q_norm, na_k_norm, na_rpb, w_fourier, mem_norm, w_mem_kv,
           mem_q_norm, mem_k_norm, out_norm, w_out, ffn_norm, w_gate, w_up, w_down):
    b, s, d = x.shape
    m = mem.shape[1]
    depth = w_in.shape[0]
    n_mem_heads = w_mem_kv.shape[2] // 2 // HEAD_DIM
    fourier_width = w_fourier.shape[1] * HEAD_DIM
    mem_width = n_mem_heads * HEAD_DIM
    na_width = (w_in.shape[2] - fourier_width - mem_width) // 3
    na_heads = na_width // HEAD_DIM
    rows = s // GRID_W
    scale = HEAD_DIM ** -0.5

    ma, mc, mch = _dft_matrices(s)
    x2 = x.reshape(b * s, d)
    mem2 = mem.reshape(b * m, d)

    for l in range(depth):
        head_gain = jnp.concatenate([
            jnp.tile(na_q_norm[l] * scale, na_heads),
            jnp.tile(na_k_norm[l], na_heads),
            jnp.ones((na_width + fourier_width,), F32),
            jnp.tile(mem_q_norm[l] * scale, n_mem_heads),
        ]).reshape(1, -1)
        q_na, k_na, v_na, u_f, q_m = _in_projection(
            x2, attn_norm[l].reshape(1, d), w_in[l].astype(BF16), head_gain)

        gain_out = out_norm[l].reshape(1, -1)
        bias = _na_bias_table(na_rpb[l], rows)
        y_na = _neighbourhood_attention(
            q_na.reshape(b, s, na_width), k_na.reshape(b, s, na_width), v_na.reshape(b, s, na_width),
            bias, gain_out[:, :na_width])

        k_m, v_m = _memory_kv(mem2, mem_norm[l].reshape(1, d), w_mem_kv[l].astype(BF16),
                              jnp.tile(mem_k_norm[l], n_mem_heads).reshape(1, -1))
        y_m = _memory_attention(q_m.reshape(b, s, mem_width), k_m.reshape(b, m, mem_width),
                                v_m.reshape(b, m, mem_width), gain_out[:, na_width + fourier_width:])

        t5 = _fft_stage_a(u_f.reshape(b, s // DFT_R1, DFT_R1, fourier_width), ma)
        y_f = _fft_stage_c(t5, mc, mch, w_fourier[l].astype(BF16),
                           gain_out[:, na_width:na_width + fourier_width])

        x2 = _out_projection(x2, y_na.reshape(b * s, na_width), y_f.reshape(b * s, fourier_width),
                             y_m.reshape(b * s, mem_width), w_out[l].astype(BF16))
        x2 = _ffn(x2, ffn_norm[l].reshape(1, d), w_gate[l].astype(BF16), w_up[l].astype(BF16),
                  w_down[l].astype(BF16))
    return x2.reshape(b, s, d)
```

```python
import functools

import numpy as np
import jax
import jax.numpy as jnp
from jax import lax
from jax.experimental import pallas as pl
from jax.experimental.pallas import tpu as pltpu

F32 = jnp.float32
BF16 = jnp.bfloat16

EPS = 1e-6
HEAD_DIM = 128
GRID_W = 64
NA_KH = 8
NA_KW = 16
NA_ROWS_PER_BLOCK = 8
NA_HALO_ROWS = 4
MASK_VALUE = -1e30
NA_INVALID_ROW = 2 * NA_KH - 1
LOG2E = 1.4426950408889634

DFT_R1 = 64
DFT_SUB = 8

MIB = 1024 * 1024
VMEM_LIMIT = 48 * MIB


def _cparams(sem, vmem=VMEM_LIMIT):
    return pltpu.CompilerParams(dimension_semantics=sem, vmem_limit_bytes=vmem)


def _rms(x, eps=EPS):
    return x * lax.rsqrt(jnp.mean(x * x, axis=-1, keepdims=True) + eps)


def _head_norm_store(dst_ref, acc, gain, n_heads, col0=0):
    for h in range(n_heads):
        sl = slice(col0 + h * HEAD_DIM, col0 + (h + 1) * HEAD_DIM)
        dl = slice(h * HEAD_DIM, (h + 1) * HEAD_DIM)
        dst_ref[:, dl] = (_rms(acc[:, sl]) * gain[:, sl]).astype(dst_ref.dtype)


def _inproj_kernel(x_ref, g_ref, w_ref, hg_ref, q_ref, k_ref, v_ref, u_ref, qm_ref, h_scr):
    j = pl.program_id(1)

    @pl.when(j == 0)
    def _():
        h_scr[...] = (_rms(x_ref[...]) * g_ref[...]).astype(BF16)

    acc = jnp.dot(h_scr[...], w_ref[...], preferred_element_type=F32)
    hg = hg_ref[...]
    n_heads = acc.shape[1] // HEAD_DIM

    @pl.when(j == 0)
    def _():
        _head_norm_store(q_ref, acc, hg, n_heads)

    @pl.when(j == 1)
    def _():
        _head_norm_store(k_ref, acc, hg, n_heads)

    @pl.when(j == 2)
    def _():
        v_ref[...] = acc.astype(v_ref.dtype)

    @pl.when(j == 3)
    def _():
        half = acc.shape[1] // 2
        u_ref[...] = acc[:, :half]
        _head_norm_store(qm_ref, acc, hg, n_heads // 2, col0=half)


def _in_projection(x2, gain, w_in, head_gain, layer, tm=1024):
    n, d = x2.shape
    d_in = w_in.shape[2]
    tn = d_in // 4
    half = tn // 2
    grid = (n // tm, 4)
    row = lambda i, j: (i, 0)
    return pl.pallas_call(
        _inproj_kernel,
        grid=grid,
        in_specs=[
            pl.BlockSpec((tm, d), row),
            pl.BlockSpec((1, d), lambda i, j: (0, 0)),
            pl.BlockSpec((None, d, tn), lambda i, j: (layer, 0, j)),
            pl.BlockSpec((1, tn), lambda i, j: (0, j)),
        ],
        out_specs=[
            pl.BlockSpec((tm, tn), row),
            pl.BlockSpec((tm, tn), row),
            pl.BlockSpec((tm, tn), row),
            pl.BlockSpec((tm, half), row),
            pl.BlockSpec((tm, half), row),
        ],
        out_shape=[
            jax.ShapeDtypeStruct((n, tn), BF16),
            jax.ShapeDtypeStruct((n, tn), BF16),
            jax.ShapeDtypeStruct((n, tn), BF16),
            jax.ShapeDtypeStruct((n, half), F32),
            jax.ShapeDtypeStruct((n, half), BF16),
        ],
        scratch_shapes=[pltpu.VMEM((tm, d), BF16)],
        compiler_params=_cparams(("parallel", "arbitrary"), vmem=56 * MIB),
        name="in_projection",
    )(x2, gain, w_in, head_gain)


def _na_build_bias(variant, e_ref, bias_scr):
    rq, halo = NA_ROWS_PER_BLOCK, NA_HALO_ROWS
    half_rows = rq // 2
    n_pairs = (half_rows + NA_KH) // 2
    lane = lax.broadcasted_iota(jnp.int32, (GRID_W, 2 * GRID_W), 1)
    for il in range(rq):
        jl0 = jnp.where(variant == 0, max(il, halo), jnp.where(variant == 2, min(il, halo), il))
        half = il // half_rows
        rows = slice((il % half_rows) * GRID_W, (il % half_rows + 1) * GRID_W)
        for pair in range(n_pairs):
            blocks = []
            for jl in (half * halo + 2 * pair, half * halo + 2 * pair + 1):
                inside = jnp.logical_and(jl >= jl0, jl < jl0 + NA_KH)
                idx = jnp.where(inside, jl - il - halo + NA_KH - 1, NA_INVALID_ROW)
                blocks.append(e_ref[0, idx])
            bias_scr[half, rows, pair * 2 * GRID_W:(pair + 1) * 2 * GRID_W] = (
                jnp.where(lane < GRID_W, blocks[0], blocks[1]))


def _na_kernel(q_ref, kp_ref, kc_ref, kn_ref, vp_ref, vc_ref, vn_ref, e_ref, g_ref, o_ref, bias_scr):
    n_batch = q_ref.shape[0]
    rb = pl.program_id(1)
    last = pl.num_programs(1) - 1
    half_tokens = q_ref.shape[1] // 2
    halo_tokens = kp_ref.shape[1]
    win_tokens = half_tokens + 2 * halo_tokens

    @pl.when(jnp.logical_or(rb <= 1, rb == last))
    def _():
        variant = jnp.where(rb == 0, 0, jnp.where(rb == last, 2, 1))
        _na_build_bias(variant, e_ref, bias_scr)

    for b in range(n_batch):
        k = jnp.concatenate([kp_ref[b], kc_ref[b], kn_ref[b]], axis=0)
        v = jnp.concatenate([vp_ref[b], vc_ref[b], vn_ref[b]], axis=0)
        for half in range(2):
            rows = slice(half * half_tokens, (half + 1) * half_tokens)
            keys = slice(half * halo_tokens, half * halo_tokens + win_tokens)
            s = lax.dot_general(q_ref[b, rows, :], k[keys], (((1,), (1,)), ((), ())),
                                preferred_element_type=F32)
            s = s + bias_scr[half]
            m = jnp.max(s, axis=-1, keepdims=True)
            p = jnp.exp2(s - m)
            l = jnp.sum(p, axis=-1, keepdims=True)
            o = jnp.dot(p.astype(BF16), v[keys], preferred_element_type=F32) * (1.0 / l)
            o_ref[b, rows, :] = (_rms(o) * g_ref[...]).astype(o_ref.dtype)


def _na_bias_blocks(rpb):
    n_heads = rpb.shape[0]
    kw = min(NA_KW, GRID_W)
    col = np.arange(GRID_W)
    col_start = np.clip(col - kw // 2, 0, GRID_W - kw)
    col_ok = (col[None, :] >= col_start[:, None]) & (col[None, :] < col_start[:, None] + kw)
    col_idx = np.clip(col[None, :] - col[:, None] + NA_KW - 1, 0, 2 * NA_KW - 2)
    toe = jnp.where(col_ok[None, None], rpb[:, :, col_idx] * LOG2E, MASK_VALUE)
    toe = jnp.concatenate([toe, jnp.full((n_heads, 1, GRID_W, GRID_W), MASK_VALUE, F32)], axis=1)
    return jnp.concatenate([toe, toe], axis=-1)


def _neighbourhood_attention(q, k, v, bias_blocks, gain):
    b, s, width = q.shape
    n_heads = width // HEAD_DIM
    tq = NA_ROWS_PER_BLOCK * GRID_W
    th = NA_HALO_ROWS * GRID_W
    n_blocks = s // tq
    n_halo_blocks = s // th
    per = tq // th
    assert n_blocks >= 3 and NA_KH == 2 * NA_HALO_ROWS == NA_ROWS_PER_BLOCK

    cur = lambda h, rb: (0, rb, h)
    prev = lambda h, rb: (0, jnp.maximum(per * rb - 1, 0), h)
    nxt = lambda h, rb: (0, jnp.minimum(per * rb + per, n_halo_blocks - 1), h)

    blk_c = pl.BlockSpec((b, tq, HEAD_DIM), cur)
    blk_p = pl.BlockSpec((b, th, HEAD_DIM), prev)
    blk_n = pl.BlockSpec((b, th, HEAD_DIM), nxt)
    return pl.pallas_call(
        _na_kernel,
        grid=(n_heads, n_blocks),
        in_specs=[
            blk_c, blk_p, blk_c, blk_n, blk_p, blk_c, blk_n,
            pl.BlockSpec((1,) + bias_blocks.shape[1:], lambda h, rb: (h, 0, 0, 0)),
            pl.BlockSpec((1, HEAD_DIM), lambda h, rb: (0, h)),
        ],
        out_specs=blk_c,
        out_shape=jax.ShapeDtypeStruct((b, s, width), BF16),
        scratch_shapes=[pltpu.VMEM((2, tq // 2, tq // 2 + 2 * th), F32)],
        compiler_params=_cparams(("arbitrary", "arbitrary")),
        name="neighbourhood_attention",
    )(q, k, k, k, v, v, v, bias_blocks, gain)


def _mem_kv_kernel(m_ref, g_ref, w_ref, kg_ref, k_ref, v_ref, h_scr):
    j = pl.program_id(0)

    @pl.when(j == 0)
    def _():
        h_scr[...] = (_rms(m_ref[...]) * g_ref[...]).astype(BF16)

    acc = jnp.dot(h_scr[...], w_ref[...], preferred_element_type=F32)

    @pl.when(j == 0)
    def _():
        _head_norm_store(k_ref, acc, kg_ref[...], acc.shape[1] // HEAD_DIM)

    @pl.when(j == 1)
    def _():
        v_ref[...] = acc.astype(v_ref.dtype)


def _memory_kv(mem2, gain, w_kv, k_gain, layer):
    n, d = mem2.shape
    width = w_kv.shape[2] // 2
    full = lambda j: (0, 0)
    return pl.pallas_call(
        _mem_kv_kernel,
        grid=(2,),
        in_specs=[
            pl.BlockSpec((n, d), full),
            pl.BlockSpec((1, d), full),
            pl.BlockSpec((None, d, width), lambda j: (layer, 0, j)),
            pl.BlockSpec((1, width), full),
        ],
        out_specs=[pl.BlockSpec((n, width), full), pl.BlockSpec((n, width), full)],
        out_shape=[jax.ShapeDtypeStruct((n, width), BF16), jax.ShapeDtypeStruct((n, width), BF16)],
        scratch_shapes=[pltpu.VMEM((n, d), BF16)],
        compiler_params=_cparams(("arbitrary",)),
        name="memory_kv",
    )(mem2, gain, w_kv, k_gain)


def _mem_attn_kernel(q_ref, k_ref, v_ref, g_ref, o_ref):
    n_heads = q_ref.shape[2] // HEAD_DIM
    for h in range(n_heads):
        sl = slice(h * HEAD_DIM, (h + 1) * HEAD_DIM)
        s = lax.dot_general(q_ref[0, :, sl], k_ref[0, :, sl], (((1,), (1,)), ((), ())),
                            preferred_element_type=F32)
        m = jnp.max(s, axis=-1, keepdims=True)
        p = jnp.exp2(s - m)
        l = jnp.sum(p, axis=-1, keepdims=True)
        o = jnp.dot(p.astype(BF16), v_ref[0, :, sl], preferred_element_type=F32) * (1.0 / l)
        o_ref[0, :, sl] = (_rms(o) * g_ref[:, sl]).astype(o_ref.dtype)


def _memory_attention(q, k, v, gain, tm=1024):
    b, s, width = q.shape
    m = k.shape[1]
    return pl.pallas_call(
        _mem_attn_kernel,
        grid=(b, s // tm),
        in_specs=[
            pl.BlockSpec((1, tm, width), lambda bi, t: (bi, t, 0)),
            pl.BlockSpec((1, m, width), lambda bi, t: (bi, 0, 0)),
            pl.BlockSpec((1, m, width), lambda bi, t: (bi, 0, 0)),
            pl.BlockSpec((1, width), lambda bi, t: (0, 0)),
        ],
        out_specs=pl.BlockSpec((1, tm, width), lambda bi, t: (bi, t, 0)),
        out_shape=jax.ShapeDtypeStruct((b, s, width), BF16),
        compiler_params=_cparams(("parallel", "arbitrary")),
        name="memory_attention",
    )(q, k, v, gain)


def _dft_matrices(seq):
    r1 = DFT_R1
    r2 = seq // r1
    sub = DFT_SUB
    k2 = np.arange(r2, dtype=np.int64)
    s2 = np.arange(r2, dtype=np.int64)
    s1 = np.arange(r1, dtype=np.int64)
    k1 = np.arange(r1, dtype=np.int64)
    th = 2.0 * np.pi * ((k2[:, None] * s2[None, :]) % r2).astype(np.float64) / r2
    fa = np.stack([np.cos(th), -np.sin(th)], axis=0)
    ma = np.einsum("cks,ab->ckasb", fa, np.eye(sub))
    ma = jnp.asarray(ma.reshape(2 * r2 * sub, r2 * sub), BF16)
    tw = 2.0 * np.pi * ((s1[:, None] * k2[None, :]) % seq).astype(np.float64) / seq
    tw = tw.reshape(r1 // sub, sub, r2).transpose(0, 2, 1).reshape(r1 // sub, r2 * sub, 1)
    tw_cos = jnp.broadcast_to(jnp.asarray(np.cos(tw), F32), (r1 // sub, r2 * sub, HEAD_DIM))
    tw_sin = jnp.broadcast_to(jnp.asarray(np.sin(tw), F32), (r1 // sub, r2 * sub, HEAD_DIM))
    ph = 2.0 * np.pi * ((k1[:, None] * s1[None, :]) % r1).astype(np.float64) / r1
    c, s = np.cos(ph), np.sin(ph)
    g = np.stack([np.stack([c, s], 0), np.stack([s, -c], 0)], 0)
    mc = np.einsum("ocks,ab->okacbs", g, np.eye(sub))
    mc = jnp.asarray(mc.reshape(2 * r1 * sub, 2 * sub * r1), BF16)
    cc = np.arange(HEAD_DIM, dtype=np.int64)
    pc = 2.0 * np.pi * ((cc[:, None] * cc[None, :]) % HEAD_DIM).astype(np.float64) / HEAD_DIM
    mch = np.concatenate([np.cos(pc), -np.sin(pc)], axis=0) / np.sqrt(float(seq) * HEAD_DIM)
    return ma, tw_cos, tw_sin, mc, jnp.asarray(mch, BF16)


def _fft_a_kernel(u_ref, ma_ref, twc_ref, tws_ref, t_ref):
    r2, sub, width = u_ref.shape[1], u_ref.shape[2], u_ref.shape[3]
    xb = u_ref[0].reshape(r2 * sub, width).astype(BF16)
    res = jnp.dot(ma_ref[...], xb, preferred_element_type=F32)
    half = r2 * sub
    tw_cos, tw_sin = twc_ref[0], tws_ref[0]
    for g in range(width // HEAD_DIM):
        sl = slice(g * HEAD_DIM, (g + 1) * HEAD_DIM)
        re, im = res[:half, sl], res[half:, sl]
        t_ref[0, 0, :, :, sl] = (re * tw_cos + im * tw_sin).reshape(r2, sub, HEAD_DIM)
        t_ref[0, 1, :, :, sl] = (im * tw_cos - re * tw_sin).reshape(r2, sub, HEAD_DIM)


def _fft_stage_a(u4, ma, tw_cos, tw_sin):
    b, r2, r1, width = u4.shape
    sub = DFT_SUB
    tw_spec = pl.BlockSpec((1, r2 * sub, HEAD_DIM), lambda t, bi: (t, 0, 0))
    return pl.pallas_call(
        _fft_a_kernel,
        grid=(r1 // sub, b),
        in_specs=[
            pl.BlockSpec((1, r2, sub, width), lambda t, bi: (bi, 0, t, 0)),
            pl.BlockSpec(ma.shape, lambda t, bi: (0, 0)),
            tw_spec, tw_spec,
        ],
        out_specs=pl.BlockSpec((1, 2, r2, sub, width), lambda t, bi: (bi, 0, 0, t, 0)),
        out_shape=jax.ShapeDtypeStruct((b, 2, r2, r1, width), F32),
        compiler_params=_cparams(("arbitrary", "arbitrary")),
        name="fourier_stage_a",
    )(u4, ma, tw_cos, tw_sin)


def _fft_c_kernel(t_ref, mc_ref, mch_ref, wf_ref, g_ref, o_ref):
    sub, r1, width = t_ref.shape[2], t_ref.shape[3], t_ref.shape[4]
    tb = t_ref[0].reshape(2 * sub * r1, width).astype(BF16)
    res = jnp.dot(mc_ref[...], tb, preferred_element_type=F32)
    half = r1 * sub
    for g in range(width // HEAD_DIM):
        sl = slice(g * HEAD_DIM, (g + 1) * HEAD_DIM)
        pq = jnp.concatenate([res[:half, sl], res[half:, sl]], axis=-1).astype(BF16)
        y = jnp.dot(pq, mch_ref[...], preferred_element_type=F32)
        z = jnp.dot(y.astype(BF16), wf_ref[g], preferred_element_type=F32)
        z = _rms(z) * g_ref[:, sl]
        o_ref[0, :, :, sl] = z.reshape(r1, sub, HEAD_DIM)


def _fft_stage_c(t5, mc, mch, w_f, gain, layer):
    b, _, r2, r1, width = t5.shape
    sub = DFT_SUB
    groups = width // HEAD_DIM
    return pl.pallas_call(
        _fft_c_kernel,
        grid=(r2 // sub, b),
        in_specs=[
            pl.BlockSpec((1, 2, sub, r1, width), lambda t, bi: (bi, 0, t, 0, 0)),
            pl.BlockSpec(mc.shape, lambda t, bi: (0, 0)),
            pl.BlockSpec(mch.shape, lambda t, bi: (0, 0)),
            pl.BlockSpec((None, groups, HEAD_DIM, HEAD_DIM), lambda t, bi: (layer, 0, 0, 0)),
            pl.BlockSpec((1, width), lambda t, bi: (0, 0)),
        ],
        out_specs=pl.BlockSpec((1, r1, sub, width), lambda t, bi: (bi, 0, t, 0)),
        out_shape=jax.ShapeDtypeStruct((b, r1, r2, width), F32),
        compiler_params=_cparams(("arbitrary", "arbitrary")),
        name="fourier_stage_c",
    )(t5, mc, mch, w_f, gain)


def _outproj_kernel(x_ref, ya_ref, yf_ref, ym_ref, w_ref, o_ref):
    na = ya_ref.shape[1]
    nf = yf_ref.shape[1]
    acc = x_ref[...]
    acc = acc + jnp.dot(ya_ref[...], w_ref[:na, :], preferred_element_type=F32)
    acc = acc + jnp.dot(yf_ref[...].astype(BF16), w_ref[na:na + nf, :], preferred_element_type=F32)
    acc = acc + jnp.dot(ym_ref[...], w_ref[na + nf:, :], preferred_element_type=F32)
    o_ref[...] = acc


def _out_projection(x2, y_na, y_f, y_m, w_out, layer, tm=1024, tn=1024):
    n, d = x2.shape
    row = lambda i, j: (i, 0)
    return pl.pallas_call(
        _outproj_kernel,
        grid=(n // tm, d // tn),
        in_specs=[
            pl.BlockSpec((tm, tn), lambda i, j: (i, j)),
            pl.BlockSpec((tm, y_na.shape[1]), row),
            pl.BlockSpec((tm, y_f.shape[1]), row),
            pl.BlockSpec((tm, y_m.shape[1]), row),
            pl.BlockSpec((None, w_out.shape[1], tn), lambda i, j: (layer, 0, j)),
        ],
        out_specs=pl.BlockSpec((tm, tn), lambda i, j: (i, j)),
        out_shape=jax.ShapeDtypeStruct((n, d), F32),
        compiler_params=_cparams(("parallel", "arbitrary")),
        name="out_projection",
    )(x2, y_na, y_f, y_m, w_out)


def _ffn_kernel(x_ref, g_ref, wg_ref, wu_ref, wd_ref, o_ref, h_scr):
    f = pl.program_id(1)

    @pl.when(f == 0)
    def _():
        x = x_ref[...]
        h_scr[...] = (_rms(x) * g_ref[...]).astype(BF16)
        o_ref[...] = x

    h = h_scr[...]
    gate = jnp.dot(h, wg_ref[...], preferred_element_type=F32)
    up = jnp.dot(h, wu_ref[...], preferred_element_type=F32)
    act = (gate * jax.nn.sigmoid(gate) * up).astype(BF16)
    o_ref[...] += jnp.dot(act, wd_ref[...], preferred_element_type=F32)


def _ffn(x2, gain, w_gate, w_up, w_down, layer, tm=512, tf=512):
    n, d = x2.shape
    d_ff = w_gate.shape[2]
    row = lambda i, f: (i, 0)
    return pl.pallas_call(
        _ffn_kernel,
        grid=(n // tm, d_ff // tf),
        in_specs=[
            pl.BlockSpec((tm, d), row),
            pl.BlockSpec((1, d), lambda i, f: (0, 0)),
            pl.BlockSpec((None, d, tf), lambda i, f: (layer, 0, f)),
            pl.BlockSpec((None, d, tf), lambda i, f: (layer, 0, f)),
            pl.BlockSpec((None, tf, d), lambda i, f: (layer, f, 0)),
        ],
        out_specs=pl.BlockSpec((tm, d), row),
        out_shape=jax.ShapeDtypeStruct((n, d), F32),
        scratch_shapes=[pltpu.VMEM((tm, d), BF16)],
        compiler_params=_cparams(("parallel", "arbitrary")),
        name="swiglu_ffn",
    )(x2, gain, w_gate, w_up, w_down)


def kernel(x, mem, attn_norm, w_in, na_q_norm, na_k_norm, na_rpb, w_fourier, mem_norm, w_mem_kv,
           mem_q_norm, mem_k_norm, out_norm, w_out, ffn_norm, w_gate, w_up, w_down):
    b, s, d = x.shape
    m = mem.shape[1]
    depth = w_in.shape[0]
    n_mem_heads = w_mem_kv.shape[2] // 2 // HEAD_DIM
    fourier_width = w_fourier.shape[1] * HEAD_DIM
    mem_width = n_mem_heads * HEAD_DIM
    na_width = (w_in.shape[2] - fourier_width - mem_width) // 3
    na_heads = na_width // HEAD_DIM
    q_scale = HEAD_DIM ** -0.5 * LOG2E

    ma, tw_cos, tw_sin, mc, mch = _dft_matrices(s)
    x2 = x.reshape(b * s, d)
    mem2 = mem.reshape(b * m, d)
    w_in_b, w_kv_b, w_f_b, w_out_b = (w.astype(BF16) for w in (w_in, w_mem_kv, w_fourier, w_out))
    w_gate_b, w_up_b, w_down_b = (w.astype(BF16) for w in (w_gate, w_up, w_down))

    for l in range(depth):
        head_gain = jnp.concatenate([
            jnp.tile(na_q_norm[l] * q_scale, na_heads),
            jnp.tile(na_k_norm[l], na_heads),
            jnp.ones((na_width + fourier_width,), F32),
            jnp.tile(mem_q_norm[l] * q_scale, n_mem_heads),
        ]).reshape(1, -1)
        q_na, k_na, v_na, u_f, q_m = _in_projection(
            x2, attn_norm[l].reshape(1, d), w_in_b, head_gain, l)

        gain_out = out_norm[l].reshape(1, -1)
        y_na = _neighbourhood_attention(
            q_na.reshape(b, s, na_width), k_na.reshape(b, s, na_width), v_na.reshape(b, s, na_width),
            _na_bias_blocks(na_rpb[l]), gain_out[:, :na_width])

        k_m, v_m = _memory_kv(mem2, mem_norm[l].reshape(1, d), w_kv_b,
                              jnp.tile(mem_k_norm[l], n_mem_heads).reshape(1, -1), l)
        y_m = _memory_attention(q_m.reshape(b, s, mem_width), k_m.reshape(b, m, mem_width),
                                v_m.reshape(b, m, mem_width), gain_out[:, na_width + fourier_width:])

        t5 = _fft_stage_a(u_f.reshape(b, s // DFT_R1, DFT_R1, fourier_width), ma, tw_cos, tw_sin)
        y_f = _fft_stage_c(t5, mc, mch, w_f_b, gain_out[:, na_width:na_width + fourier_width], l)

        x2 = _out_projection(x2, y_na.reshape(b * s, na_width), y_f.reshape(b * s, fourier_width),
                             y_m.reshape(b * s, mem_width), w_out_b, l)
        x2 = _ffn(x2, ffn_norm[l].reshape(1, d), w_gate_b, w_up_b, w_down_b, l)
    return x2.reshape(b, s, d)
```

```python
import functools

import numpy as np
import jax
import jax.numpy as jnp
from jax import lax
from jax.experimental import pallas as pl
from jax.experimental.pallas import tpu as pltpu

F32 = jnp.float32
BF16 = jnp.bfloat16

EPS = 1e-6
HEAD_DIM = 128
MXU_N = 256
GRID_W = 64
NA_KH = 8
NA_KW = 16
NA_ROWS_PER_BLOCK = 8
NA_HALO_ROWS = 4
NA_HEADS_PER_STEP = 2
MASK_VALUE = -1e30
NA_INVALID_ROW = 2 * NA_KH - 1
LOG2E = 1.4426950408889634

DFT_R1 = 64
DFT_SUB = 8

MIB = 1024 * 1024
VMEM_LIMIT = 48 * MIB


def _cparams(sem, vmem=VMEM_LIMIT):
    return pltpu.CompilerParams(dimension_semantics=sem, vmem_limit_bytes=vmem)


def _rms(x, eps=EPS):
    return x * lax.rsqrt(jnp.mean(x * x, axis=-1, keepdims=True) + eps)


def _head_norm_store(dst_ref, acc, gain, n_heads, col0=0):
    for h in range(n_heads):
        sl = slice(col0 + h * HEAD_DIM, col0 + (h + 1) * HEAD_DIM)
        dl = slice(h * HEAD_DIM, (h + 1) * HEAD_DIM)
        dst_ref[:, dl] = (_rms(acc[:, sl]) * gain[:, sl]).astype(dst_ref.dtype)


def _inproj_kernel(x_ref, g_ref, w_ref, hg_ref, nf_ref, p_ref, u_ref, h_scr):
    j = pl.program_id(1)

    @pl.when(j == 0)
    def _():
        h_scr[...] = (_rms(x_ref[...]) * g_ref[...]).astype(BF16)

    h = h_scr[...]
    for c in range(w_ref.shape[1] // MXU_N):
        acc = jnp.dot(h, w_ref[:, c * MXU_N:(c + 1) * MXU_N], preferred_element_type=F32)
        for hh in range(MXU_N // HEAD_DIM):
            blk = acc[:, hh * HEAD_DIM:(hh + 1) * HEAD_DIM]
            sl = slice(c * MXU_N + hh * HEAD_DIM, c * MXU_N + (hh + 1) * HEAD_DIM)
            inv = lax.rsqrt(jnp.mean(blk * blk, axis=-1, keepdims=True) + EPS)
            scale = jnp.where(nf_ref[:, sl] > 0.0, inv, 1.0)
            p_ref[:, sl] = (blk * scale * hg_ref[:, sl]).astype(p_ref.dtype)
            if sl.stop <= u_ref.shape[1]:
                u_ref[:, sl] = blk


def _in_projection(x2, gain, w_in, head_gain, norm_flag, layer, u_col, u_width, tm=1024, tn=1024):
    n, d = x2.shape
    d_in = w_in.shape[2]
    assert u_col == d_in - tn and u_width <= tn
    return pl.pallas_call(
        _inproj_kernel,
        grid=(n // tm, d_in // tn),
        in_specs=[
            pl.BlockSpec((tm, d), lambda i, j: (i, 0)),
            pl.BlockSpec((1, d), lambda i, j: (0, 0)),
            pl.BlockSpec((None, d, tn), lambda i, j: (layer, 0, j)),
            pl.BlockSpec((1, tn), lambda i, j: (0, j)),
            pl.BlockSpec((1, tn), lambda i, j: (0, j)),
        ],
        out_specs=[
            pl.BlockSpec((tm, tn), lambda i, j: (i, j)),
            pl.BlockSpec((tm, u_width), lambda i, j: (i, 0)),
        ],
        out_shape=[
            jax.ShapeDtypeStruct((n, d_in), BF16),
            jax.ShapeDtypeStruct((n, u_width), F32),
        ],
        scratch_shapes=[pltpu.VMEM((tm, d), BF16)],
        compiler_params=_cparams(("parallel", "arbitrary")),
        name="in_projection",
    )(x2, gain, w_in, head_gain, norm_flag)


def _na_build_bias(variant, e_ref, bias_scr, hh):
    rq, halo = NA_ROWS_PER_BLOCK, NA_HALO_ROWS
    half_rows = rq // 2
    n_pairs = (half_rows + NA_KH) // 2
    lane = lax.broadcasted_iota(jnp.int32, (GRID_W, 2 * GRID_W), 1)
    for il in range(rq):
        jl0 = jnp.where(variant == 0, max(il, halo), jnp.where(variant == 2, min(il, halo), il))
        half = il // half_rows
        rows = slice((il % half_rows) * GRID_W, (il % half_rows + 1) * GRID_W)
        for pair in range(n_pairs):
            blocks = []
            for jl in (half * halo + 2 * pair, half * halo + 2 * pair + 1):
                inside = jnp.logical_and(jl >= jl0, jl < jl0 + NA_KH)
                idx = jnp.where(inside, jl - il - halo + NA_KH - 1, NA_INVALID_ROW)
                blocks.append(e_ref[hh, idx])
            bias_scr[hh, half, rows, pair * 2 * GRID_W:(pair + 1) * 2 * GRID_W] = (
                jnp.where(lane < GRID_W, blocks[0], blocks[1]))


def _na_kernel(q_ref, kp_ref, kc_ref, kn_ref, vp_ref, vc_ref, vn_ref, e_ref, g_ref, o_ref, bias_scr):
    n_batch = q_ref.shape[0]
    n_heads = q_ref.shape[2] // HEAD_DIM
    rb = pl.program_id(1)
    last = pl.num_programs(1) - 1
    half_tokens = q_ref.shape[1] // 2
    contract_last = (((1,), (1,)), ((), ()))

    @pl.when(jnp.logical_or(rb <= 1, rb == last))
    def _():
        variant = jnp.where(rb == 0, 0, jnp.where(rb == last, 2, 1))
        for hh in range(n_heads):
            _na_build_bias(variant, e_ref, bias_scr, hh)

    segments = (((kp_ref, vp_ref), (kc_ref, vc_ref)), ((kc_ref, vc_ref), (kn_ref, vn_ref)))
    for hh in range(n_heads):
        hs = slice(hh * HEAD_DIM, (hh + 1) * HEAD_DIM)
        for b in range(n_batch):
            for half in range(2):
                rows = slice(half * half_tokens, (half + 1) * half_tokens)
                q = q_ref[b, rows, hs]
                s = jnp.concatenate(
                    [lax.dot_general(q, k_ref[b, :, hs], contract_last, preferred_element_type=F32)
                     for k_ref, _ in segments[half]], axis=-1)
                s = s + bias_scr[hh, half]
                m = jnp.max(s, axis=-1, keepdims=True)
                p = jnp.exp2(s - m)
                l = jnp.sum(p, axis=-1, keepdims=True)
                pb = p.astype(BF16)
                o, col = None, 0
                for _, v_ref in segments[half]:
                    n_keys = v_ref.shape[1]
                    part = jnp.dot(pb[:, col:col + n_keys], v_ref[b, :, hs], preferred_element_type=F32)
                    o = part if o is None else o + part
                    col += n_keys
                o = o * (1.0 / l)
                o_ref[b, rows, hs] = (_rms(o) * g_ref[:, hs]).astype(o_ref.dtype)


def _na_bias_blocks(rpb):
    n_heads = rpb.shape[0]
    kw = min(NA_KW, GRID_W)
    col = np.arange(GRID_W)
    col_start = np.clip(col - kw // 2, 0, GRID_W - kw)
    col_ok = (col[None, :] >= col_start[:, None]) & (col[None, :] < col_start[:, None] + kw)
    col_idx = np.clip(col[None, :] - col[:, None] + NA_KW - 1, 0, 2 * NA_KW - 2)
    toe = jnp.where(col_ok[None, None], rpb[:, :, col_idx] * LOG2E, MASK_VALUE)
    toe = jnp.concatenate([toe, jnp.full((n_heads, 1, GRID_W, GRID_W), MASK_VALUE, F32)], axis=1)
    return jnp.concatenate([toe, toe], axis=-1)


def _neighbourhood_attention(proj, bias_blocks, gain, n_heads):
    b, s, _ = proj.shape
    width = n_heads * HEAD_DIM
    tq = NA_ROWS_PER_BLOCK * GRID_W
    th = NA_HALO_ROWS * GRID_W
    n_blocks = s // tq
    n_halo_blocks = s // th
    per = tq // th
    assert n_blocks >= 3 and NA_KH == 2 * NA_HALO_ROWS == NA_ROWS_PER_BLOCK

    hps = NA_HEADS_PER_STEP
    wb = hps * HEAD_DIM
    assert n_heads % hps == 0

    def cur(col0):
        return pl.BlockSpec((b, tq, wb), lambda h, rb: (0, rb, col0 + h))

    def prev(col0):
        return pl.BlockSpec((b, th, wb), lambda h, rb: (0, jnp.maximum(per * rb - 1, 0), col0 + h))

    def nxt(col0):
        return pl.BlockSpec((b, th, wb),
                            lambda h, rb: (0, jnp.minimum(per * rb + per, n_halo_blocks - 1), col0 + h))

    k0, v0 = n_heads // hps, 2 * n_heads // hps
    return pl.pallas_call(
        _na_kernel,
        grid=(n_heads // hps, n_blocks),
        in_specs=[
            cur(0), prev(k0), cur(k0), nxt(k0), prev(v0), cur(v0), nxt(v0),
            pl.BlockSpec((hps,) + bias_blocks.shape[1:], lambda h, rb: (h, 0, 0, 0)),
            pl.BlockSpec((1, wb), lambda h, rb: (0, h)),
        ],
        out_specs=cur(0),
        out_shape=jax.ShapeDtypeStruct((b, s, width), BF16),
        scratch_shapes=[pltpu.VMEM((hps, 2, tq // 2, tq // 2 + 2 * th), F32)],
        compiler_params=_cparams(("arbitrary", "arbitrary")),
        name="neighbourhood_attention",
    )(proj, proj, proj, proj, proj, proj, proj, bias_blocks, gain)


def _mem_kv_kernel(m_ref, g_ref, w_ref, kg_ref, k_ref, v_ref, h_scr):
    j = pl.program_id(0)

    @pl.when(j == 0)
    def _():
        h_scr[...] = (_rms(m_ref[...]) * g_ref[...]).astype(BF16)

    acc = jnp.dot(h_scr[...], w_ref[...], preferred_element_type=F32)

    @pl.when(j == 0)
    def _():
        _head_norm_store(k_ref, acc, kg_ref[...], acc.shape[1] // HEAD_DIM)

    @pl.when(j == 1)
    def _():
        v_ref[...] = acc.astype(v_ref.dtype)


def _memory_kv(mem2, gain, w_kv, k_gain, layer):
    n, d = mem2.shape
    width = w_kv.shape[2] // 2
    full = lambda j: (0, 0)
    return pl.pallas_call(
        _mem_kv_kernel,
        grid=(2,),
        in_specs=[
            pl.BlockSpec((n, d), full),
            pl.BlockSpec((1, d), full),
            pl.BlockSpec((None, d, width), lambda j: (layer, 0, j)),
            pl.BlockSpec((1, width), full),
        ],
        out_specs=[pl.BlockSpec((n, width), full), pl.BlockSpec((n, width), full)],
        out_shape=[jax.ShapeDtypeStruct((n, width), BF16), jax.ShapeDtypeStruct((n, width), BF16)],
        scratch_shapes=[pltpu.VMEM((n, d), BF16)],
        compiler_params=_cparams(("arbitrary",)),
        name="memory_kv",
    )(mem2, gain, w_kv, k_gain)


def _mem_attn_kernel(q_ref, k_ref, v_ref, g_ref, o_ref):
    n_heads = q_ref.shape[2] // HEAD_DIM
    for h in range(n_heads):
        sl = slice(h * HEAD_DIM, (h + 1) * HEAD_DIM)
        s = lax.dot_general(q_ref[0, :, sl], k_ref[0, :, sl], (((1,), (1,)), ((), ())),
                            preferred_element_type=F32)
        m = jnp.max(s, axis=-1, keepdims=True)
        p = jnp.exp2(s - m)
        l = jnp.sum(p, axis=-1, keepdims=True)
        o = jnp.dot(p.astype(BF16), v_ref[0, :, sl], preferred_element_type=F32) * (1.0 / l)
        o_ref[0, :, sl] = (_rms(o) * g_ref[:, sl]).astype(o_ref.dtype)


def _memory_attention(proj, q_col, k, v, gain, tm=1024):
    b, s, _ = proj.shape
    m, width = k.shape[1], k.shape[2]
    assert q_col % width == 0
    return pl.pallas_call(
        _mem_attn_kernel,
        grid=(b, s // tm),
        in_specs=[
            pl.BlockSpec((1, tm, width), lambda bi, t: (bi, t, q_col // width)),
            pl.BlockSpec((1, m, width), lambda bi, t: (bi, 0, 0)),
            pl.BlockSpec((1, m, width), lambda bi, t: (bi, 0, 0)),
            pl.BlockSpec((1, width), lambda bi, t: (0, 0)),
        ],
        out_specs=pl.BlockSpec((1, tm, width), lambda bi, t: (bi, t, 0)),
        out_shape=jax.ShapeDtypeStruct((b, s, width), BF16),
        compiler_params=_cparams(("parallel", "arbitrary")),
        name="memory_attention",
    )(proj, k, v, gain)


def _dft_matrices(seq):
    r1 = DFT_R1
    r2 = seq // r1
    sub = DFT_SUB
    k2 = np.arange(r2, dtype=np.int64)
    s2 = np.arange(r2, dtype=np.int64)
    s1 = np.arange(r1, dtype=np.int64)
    k1 = np.arange(r1, dtype=np.int64)
    th = 2.0 * np.pi * ((k2[:, None] * s2[None, :]) % r2).astype(np.float64) / r2
    fa = np.stack([np.cos(th), -np.sin(th)], axis=0)
    ma = np.einsum("cks,ab->ckasb", fa, np.eye(sub))
    ma = jnp.asarray(ma.reshape(2 * r2 * sub, r2 * sub), BF16)
    tw = 2.0 * np.pi * ((s1[:, None] * k2[None, :]) % seq).astype(np.float64) / seq
    tw = tw.reshape(r1 // sub, sub, r2).transpose(0, 2, 1).reshape(r1 // sub, r2 * sub, 1)
    tw_cos = jnp.broadcast_to(jnp.asarray(np.cos(tw), F32), (r1 // sub, r2 * sub, HEAD_DIM))
    tw_sin = jnp.broadcast_to(jnp.asarray(np.sin(tw), F32), (r1 // sub, r2 * sub, HEAD_DIM))
    ph = 2.0 * np.pi * ((k1[:, None] * s1[None, :]) % r1).astype(np.float64) / r1
    c, s = np.cos(ph), np.sin(ph)
    g = np.stack([np.stack([c, s], 0), np.stack([s, -c], 0)], 0)
    mc = np.einsum("ocks,ab->okacbs", g, np.eye(sub))
    mc = jnp.asarray(mc.reshape(2 * r1 * sub, 2 * sub * r1), BF16)
    cc = np.arange(HEAD_DIM, dtype=np.int64)
    pc = 2.0 * np.pi * ((cc[:, None] * cc[None, :]) % HEAD_DIM).astype(np.float64) / HEAD_DIM
    mch = np.concatenate([np.cos(pc), -np.sin(pc)], axis=0) / np.sqrt(float(seq) * HEAD_DIM)
    return ma, tw_cos, tw_sin, mc, jnp.asarray(mch, BF16)


def _fft_a_kernel(u_ref, ma_ref, twc_ref, tws_ref, t_ref):
    r2, sub, width = u_ref.shape[1], u_ref.shape[2], u_ref.shape[3]
    xb = u_ref[0].reshape(r2 * sub, width).astype(BF16)
    res = jnp.dot(ma_ref[...], xb, preferred_element_type=F32)
    half = r2 * sub
    tw_cos, tw_sin = twc_ref[0], tws_ref[0]
    for g in range(width // HEAD_DIM):
        sl = slice(g * HEAD_DIM, (g + 1) * HEAD_DIM)
        re, im = res[:half, sl], res[half:, sl]
        t_ref[0, 0, :, :, sl] = (re * tw_cos + im * tw_sin).reshape(r2, sub, HEAD_DIM)
        t_ref[0, 1, :, :, sl] = (im * tw_cos - re * tw_sin).reshape(r2, sub, HEAD_DIM)


def _fft_stage_a(u4, ma, tw_cos, tw_sin):
    b, r2, r1, width = u4.shape
    sub = DFT_SUB
    tw_spec = pl.BlockSpec((1, r2 * sub, HEAD_DIM), lambda t, bi: (t, 0, 0))
    return pl.pallas_call(
        _fft_a_kernel,
        grid=(r1 // sub, b),
        in_specs=[
            pl.BlockSpec((1, r2, sub, width), lambda t, bi: (bi, 0, t, 0)),
            pl.BlockSpec(ma.shape, lambda t, bi: (0, 0)),
            tw_spec, tw_spec,
        ],
        out_specs=pl.BlockSpec((1, 2, r2, sub, width), lambda t, bi: (bi, 0, 0, t, 0)),
        out_shape=jax.ShapeDtypeStruct((b, 2, r2, r1, width), F32),
        compiler_params=_cparams(("arbitrary", "arbitrary")),
        name="fourier_stage_a",
    )(u4, ma, tw_cos, tw_sin)


def _fft_c_kernel(t_ref, mc_ref, mch_ref, wf_ref, g_ref, o_ref):
    sub, r1, width = t_ref.shape[2], t_ref.shape[3], t_ref.shape[4]
    tb = t_ref[0].reshape(2 * sub * r1, width).astype(BF16)
    res = jnp.dot(mc_ref[...], tb, preferred_element_type=F32)
    half = r1 * sub
    for g in range(width // HEAD_DIM):
        sl = slice(g * HEAD_DIM, (g + 1) * HEAD_DIM)
        pq = jnp.concatenate([res[:half, sl], res[half:, sl]], axis=-1).astype(BF16)
        y = jnp.dot(pq, mch_ref[...], preferred_element_type=F32)
        z = jnp.dot(y.astype(BF16), wf_ref[g], preferred_element_type=F32)
        z = _rms(z) * g_ref[:, sl]
        o_ref[0, :, :, sl] = z.reshape(r1, sub, HEAD_DIM)


def _fft_stage_c(t5, mc, mch, w_f, gain, layer):
    b, _, r2, r1, width = t5.shape
    sub = DFT_SUB
    groups = width // HEAD_DIM
    return pl.pallas_call(
        _fft_c_kernel,
        grid=(r2 // sub, b),
        in_specs=[
            pl.BlockSpec((1, 2, sub, r1, width), lambda t, bi: (bi, 0, t, 0, 0)),
            pl.BlockSpec(mc.shape, lambda t, bi: (0, 0)),
            pl.BlockSpec(mch.shape, lambda t, bi: (0, 0)),
            pl.BlockSpec((None, groups, HEAD_DIM, HEAD_DIM), lambda t, bi: (layer, 0, 0, 0)),
            pl.BlockSpec((1, width), lambda t, bi: (0, 0)),
        ],
        out_specs=pl.BlockSpec((1, r1, sub, width), lambda t, bi: (bi, 0, t, 0)),
        out_shape=jax.ShapeDtypeStruct((b, r1, r2, width), F32),
        compiler_params=_cparams(("arbitrary", "arbitrary")),
        name="fourier_stage_c",
    )(t5, mc, mch, w_f, gain)


def _outproj_kernel(x_ref, ya_ref, yf_ref, ym_ref, w_ref, o_ref):
    na = ya_ref.shape[1]
    nf = yf_ref.shape[1]
    acc = x_ref[...]
    acc = acc + jnp.dot(ya_ref[...], w_ref[:na, :], preferred_element_type=F32)
    acc = acc + jnp.dot(yf_ref[...].astype(BF16), w_ref[na:na + nf, :], preferred_element_type=F32)
    acc = acc + jnp.dot(ym_ref[...], w_ref[na + nf:, :], preferred_element_type=F32)
    o_ref[...] = acc


def _out_projection(x2, y_na, y_f, y_m, w_out, layer, tm=512):
    n, d = x2.shape
    row = lambda i: (i, 0)
    return pl.pallas_call(
        _outproj_kernel,
        grid=(n // tm,),
        in_specs=[
            pl.BlockSpec((tm, d), row),
            pl.BlockSpec((tm, y_na.shape[1]), row),
            pl.BlockSpec((tm, y_f.shape[1]), row),
            pl.BlockSpec((tm, y_m.shape[1]), row),
            pl.BlockSpec((None,) + w_out.shape[1:], lambda i: (layer, 0, 0)),
        ],
        out_specs=pl.BlockSpec((tm, d), row),
        out_shape=jax.ShapeDtypeStruct((n, d), F32),
        compiler_params=_cparams(("parallel",)),
        name="out_projection",
    )(x2, y_na, y_f, y_m, w_out)


def _ffn_kernel(x_ref, g_ref, wg_ref, wu_ref, wd_ref, o_ref, h_scr):
    f = pl.program_id(1)

    @pl.when(f == 0)
    def _():
        x = x_ref[...]
        h_scr[...] = (_rms(x) * g_ref[...]).astype(BF16)
        o_ref[...] = x

    h = h_scr[...]
    gate = jnp.dot(h, wg_ref[...], preferred_element_type=F32)
    up = jnp.dot(h, wu_ref[...], preferred_element_type=F32)
    act = (gate * jax.nn.sigmoid(gate) * up).astype(BF16)
    o_ref[...] += jnp.dot(act, wd_ref[...], preferred_element_type=F32)


def _ffn(x2, gain, w_gate, w_up, w_down, layer, tm=1024, tf=512):
    n, d = x2.shape
    d_ff = w_gate.shape[2]
    row = lambda i, f: (i, 0)
    return pl.pallas_call(
        _ffn_kernel,
        grid=(n // tm, d_ff // tf),
        in_specs=[
            pl.BlockSpec((tm, d), row, pipeline_mode=pl.Buffered(1)),
            pl.BlockSpec((1, d), lambda i, f: (0, 0)),
            pl.BlockSpec((None, d, tf), lambda i, f: (layer, 0, f)),
            pl.BlockSpec((None, d, tf), lambda i, f: (layer, 0, f)),
            pl.BlockSpec((None, tf, d), lambda i, f: (layer, f, 0)),
        ],
        out_specs=pl.BlockSpec((tm, d), row),
        out_shape=jax.ShapeDtypeStruct((n, d), F32),
        scratch_shapes=[pltpu.VMEM((tm, d), BF16)],
        compiler_params=_cparams(("parallel", "arbitrary"), vmem=56 * MIB),
        name="swiglu_ffn",
    )(x2, gain, w_gate, w_up, w_down)


def kernel(x, mem, attn_norm, w_in, na_q_norm, na_k_norm, na_rpb, w_fourier, mem_norm, w_mem_kv,
           mem_q_norm, mem_k_norm, out_norm, w_out, ffn_norm, w_gate, w_up, w_down):
    b, s, d = x.shape
    m = mem.shape[1]
    depth = w_in.shape[0]
    n_mem_heads = w_mem_kv.shape[2] // 2 // HEAD_DIM
    fourier_width = w_fourier.shape[1] * HEAD_DIM
    mem_width = n_mem_heads * HEAD_DIM
    na_width = (w_in.shape[2] - fourier_width - mem_width) // 3
    na_heads = na_width // HEAD_DIM
    q_scale = HEAD_DIM ** -0.5 * LOG2E

    ma, tw_cos, tw_sin, mc, mch = _dft_matrices(s)
    x2 = x.reshape(b * s, d)
    mem2 = mem.reshape(b * m, d)
    w_in_b, w_kv_b, w_f_b, w_out_b = (w.astype(BF16) for w in (w_in, w_mem_kv, w_fourier, w_out))
    w_gate_b, w_up_b, w_down_b = (w.astype(BF16) for w in (w_gate, w_up, w_down))
    norm_flag = jnp.asarray(np.concatenate([
        np.ones(2 * na_width), np.zeros(na_width + fourier_width), np.ones(mem_width)]), F32).reshape(1, -1)

    for l in range(depth):
        head_gain = jnp.concatenate([
            jnp.tile(na_q_norm[l] * q_scale, na_heads),
            jnp.tile(na_k_norm[l], na_heads),
            jnp.ones((na_width + fourier_width,), F32),
            jnp.tile(mem_q_norm[l] * q_scale, n_mem_heads),
        ]).reshape(1, -1)
        proj, u_f = _in_projection(x2, attn_norm[l].reshape(1, d), w_in_b, head_gain, norm_flag, l,
                                   u_col=3 * na_width, u_width=fourier_width)
        proj = proj.reshape(b, s, -1)

        gain_out = out_norm[l].reshape(1, -1)
        y_na = _neighbourhood_attention(proj, _na_bias_blocks(na_rpb[l]), gain_out[:, :na_width], na_heads)

        k_m, v_m = _memory_kv(mem2, mem_norm[l].reshape(1, d), w_kv_b,
                              jnp.tile(mem_k_norm[l], n_mem_heads).reshape(1, -1), l)
        y_m = _memory_attention(proj, 3 * na_width + fourier_width, k_m.reshape(b, m, mem_width),
                                v_m.reshape(b, m, mem_width), gain_out[:, na_width + fourier_width:])

        t5 = _fft_stage_a(u_f.reshape(b, s // DFT_R1, DFT_R1, fourier_width), ma, tw_cos, tw_sin)
        y_f = _fft_stage_c(t5, mc, mch, w_f_b, gain_out[:, na_width:na_width + fourier_width], l)

        x2 = _out_projection(x2, y_na.reshape(b * s, na_width), y_f.reshape(b * s, fourier_width),
                             y_m.reshape(b * s, mem_width), w_out_b, l)
        x2 = _ffn(x2, ffn_norm[l].reshape(1, d), w_gate_b, w_up_b, w_down_b, l)
    return x2.reshape(b, s, d)
```

```python
import functools

import numpy as np
import jax
import jax.numpy as jnp
from jax import lax
from jax.experimental import pallas as pl
from jax.experimental.pallas import tpu as pltpu

F32 = jnp.float32
BF16 = jnp.bfloat16

EPS = 1e-6
HEAD_DIM = 128
MXU_N = 256
GRID_W = 64
NA_KH = 8
NA_KW = 16
NA_ROWS_PER_BLOCK = 8
NA_HALO_ROWS = 4
NA_HEADS_PER_STEP = 2
MASK_VALUE = -1e30
NA_INVALID_ROW = 2 * NA_KH - 1
LOG2E = 1.4426950408889634

DFT_R1 = 64
DFT_SUB = 8

MIB = 1024 * 1024
VMEM_LIMIT = 48 * MIB


def _cparams(sem, vmem=VMEM_LIMIT):
    return pltpu.CompilerParams(dimension_semantics=sem, vmem_limit_bytes=vmem)


def _rms(x, eps=EPS):
    return x * lax.rsqrt(jnp.mean(x * x, axis=-1, keepdims=True) + eps)


def _head_norm_store(dst_ref, acc, gain, n_heads, col0=0):
    for h in range(n_heads):
        sl = slice(col0 + h * HEAD_DIM, col0 + (h + 1) * HEAD_DIM)
        dl = slice(h * HEAD_DIM, (h + 1) * HEAD_DIM)
        dst_ref[:, dl] = (_rms(acc[:, sl]) * gain[:, sl]).astype(dst_ref.dtype)


def _inproj_kernel(x_ref, g_ref, w_ref, hg_ref, nf_ref, p_ref, u_ref, h_scr):
    j = pl.program_id(1)

    @pl.when(j == 0)
    def _():
        h_scr[...] = (_rms(x_ref[...]) * g_ref[...]).astype(BF16)

    h = h_scr[...]
    for c in range(w_ref.shape[1] // MXU_N):
        acc = jnp.dot(h, w_ref[:, c * MXU_N:(c + 1) * MXU_N], preferred_element_type=F32)
        for hh in range(MXU_N // HEAD_DIM):
            blk = acc[:, hh * HEAD_DIM:(hh + 1) * HEAD_DIM]
            sl = slice(c * MXU_N + hh * HEAD_DIM, c * MXU_N + (hh + 1) * HEAD_DIM)
            inv = lax.rsqrt(jnp.mean(blk * blk, axis=-1, keepdims=True) + EPS)
            scale = jnp.where(nf_ref[:, sl] > 0.0, inv, 1.0)
            p_ref[:, sl] = (blk * scale * hg_ref[:, sl]).astype(p_ref.dtype)
            if sl.stop <= u_ref.shape[1]:
                u_ref[:, sl] = blk


def _in_projection(x2, gain, w_in, head_gain, norm_flag, layer, u_col, u_width, tm=1024, tn=1024):
    n, d = x2.shape
    d_in = w_in.shape[2]
    assert u_col == d_in - tn and u_width <= tn
    return pl.pallas_call(
        _inproj_kernel,
        grid=(n // tm, d_in // tn),
        in_specs=[
            pl.BlockSpec((tm, d), lambda i, j: (i, 0)),
            pl.BlockSpec((1, d), lambda i, j: (0, 0)),
            pl.BlockSpec((None, d, tn), lambda i, j: (layer, 0, j)),
            pl.BlockSpec((1, tn), lambda i, j: (0, j)),
            pl.BlockSpec((1, tn), lambda i, j: (0, j)),
        ],
        out_specs=[
            pl.BlockSpec((tm, tn), lambda i, j: (i, j)),
            pl.BlockSpec((tm, u_width), lambda i, j: (i, 0)),
        ],
        out_shape=[
            jax.ShapeDtypeStruct((n, d_in), BF16),
            jax.ShapeDtypeStruct((n, u_width), F32),
        ],
        scratch_shapes=[pltpu.VMEM((tm, d), BF16)],
        compiler_params=_cparams(("parallel", "arbitrary")),
        name="in_projection",
    )(x2, gain, w_in, head_gain, norm_flag)


def _na_build_bias(variant, e_ref, bias_scr, hh):
    rq, halo = NA_ROWS_PER_BLOCK, NA_HALO_ROWS
    half_rows = rq // 2
    n_pairs = (half_rows + NA_KH) // 2
    lane = lax.broadcasted_iota(jnp.int32, (GRID_W, 2 * GRID_W), 1)
    for il in range(rq):
        jl0 = jnp.where(variant == 0, max(il, halo), jnp.where(variant == 2, min(il, halo), il))
        half = il // half_rows
        rows = slice((il % half_rows) * GRID_W, (il % half_rows + 1) * GRID_W)
        for pair in range(n_pairs):
            blocks = []
            for jl in (half * halo + 2 * pair, half * halo + 2 * pair + 1):
                inside = jnp.logical_and(jl >= jl0, jl < jl0 + NA_KH)
                idx = jnp.where(inside, jl - il - halo + NA_KH - 1, NA_INVALID_ROW)
                blocks.append(e_ref[hh, idx])
            bias_scr[hh, half, rows, pair * 2 * GRID_W:(pair + 1) * 2 * GRID_W] = (
                jnp.where(lane < GRID_W, blocks[0], blocks[1]))


def _na_kernel(q_ref, kp_ref, kc_ref, kn_ref, vp_ref, vc_ref, vn_ref, e_ref, g_ref, o_ref,
               bias_scr, vx_scr):
    n_batch = q_ref.shape[0]
    n_heads = q_ref.shape[2] // HEAD_DIM
    rb = pl.program_id(1)
    last = pl.num_programs(1) - 1
    half_tokens = q_ref.shape[1] // 2
    halo_tokens = kp_ref.shape[1]
    win_tokens = half_tokens + 2 * halo_tokens
    contract_last = (((1,), (1,)), ((), ()))

    @pl.when(jnp.logical_or(rb <= 1, rb == last))
    def _():
        variant = jnp.where(rb == 0, 0, jnp.where(rb == last, 2, 1))
        for hh in range(n_heads):
            _na_build_bias(variant, e_ref, bias_scr, hh)

    for hh in range(n_heads):
        hs = slice(hh * HEAD_DIM, (hh + 1) * HEAD_DIM)
        for b in range(n_batch):
            row0 = 0
            for v_ref in (vp_ref, vc_ref, vn_ref):
                vx_scr[b, hh, row0:row0 + v_ref.shape[1], :HEAD_DIM] = v_ref[b, :, hs]
                row0 += v_ref.shape[1]
            vx_scr[b, hh, :, HEAD_DIM:] = jnp.ones((row0, HEAD_DIM), BF16)

    k_segments = ((kp_ref, kc_ref), (kc_ref, kn_ref))
    for hh in range(n_heads):
        hs = slice(hh * HEAD_DIM, (hh + 1) * HEAD_DIM)
        for b in range(n_batch):
            for half in range(2):
                rows = slice(half * half_tokens, (half + 1) * half_tokens)
                keys = slice(half * halo_tokens, half * halo_tokens + win_tokens)
                q = q_ref[b, rows, hs]
                s = jnp.concatenate(
                    [lax.dot_general(q, k_ref[b, :, hs], contract_last, preferred_element_type=F32)
                     for k_ref in k_segments[half]], axis=-1)
                s = s + bias_scr[hh, half]
                m = jnp.max(s, axis=-1, keepdims=True)
                pb = jnp.exp2((s - m).astype(BF16))
                ol = jnp.dot(pb, vx_scr[b, hh, keys, :], preferred_element_type=F32)
                o = ol[:, :HEAD_DIM] * (1.0 / ol[:, HEAD_DIM:])
                o_ref[b, rows, hs] = (_rms(o) * g_ref[:, hs]).astype(o_ref.dtype)


def _na_bias_blocks(rpb):
    n_heads = rpb.shape[0]
    kw = min(NA_KW, GRID_W)
    col = np.arange(GRID_W)
    col_start = np.clip(col - kw // 2, 0, GRID_W - kw)
    col_ok = (col[None, :] >= col_start[:, None]) & (col[None, :] < col_start[:, None] + kw)
    col_idx = np.clip(col[None, :] - col[:, None] + NA_KW - 1, 0, 2 * NA_KW - 2)
    onehot = jnp.asarray(col_idx[None] == np.arange(2 * NA_KW - 1)[:, None, None], F32)
    picked = jnp.einsum("hde,eqk->hdqk", rpb, onehot, precision=lax.Precision.HIGHEST)
    toe = jnp.where(col_ok[None, None], picked * LOG2E, MASK_VALUE)
    toe = jnp.concatenate([toe, jnp.full((n_heads, 1, GRID_W, GRID_W), MASK_VALUE, F32)], axis=1)
    return jnp.concatenate([toe, toe], axis=-1)


def _neighbourhood_attention(proj, bias_blocks, gain, n_heads):
    b, s, _ = proj.shape
    width = n_heads * HEAD_DIM
    tq = NA_ROWS_PER_BLOCK * GRID_W
    th = NA_HALO_ROWS * GRID_W
    n_blocks = s // tq
    n_halo_blocks = s // th
    per = tq // th
    assert n_blocks >= 3 and NA_KH == 2 * NA_HALO_ROWS == NA_ROWS_PER_BLOCK

    hps = NA_HEADS_PER_STEP
    wb = hps * HEAD_DIM
    assert n_heads % hps == 0

    def cur(col0):
        return pl.BlockSpec((b, tq, wb), lambda h, rb: (0, rb, col0 + h))

    def prev(col0):
        return pl.BlockSpec((b, th, wb), lambda h, rb: (0, jnp.maximum(per * rb - 1, 0), col0 + h))

    def nxt(col0):
        return pl.BlockSpec((b, th, wb),
                            lambda h, rb: (0, jnp.minimum(per * rb + per, n_halo_blocks - 1), col0 + h))

    k0, v0 = n_heads // hps, 2 * n_heads // hps
    return pl.pallas_call(
        _na_kernel,
        grid=(n_heads // hps, n_blocks),
        in_specs=[
            cur(0), prev(k0), cur(k0), nxt(k0), prev(v0), cur(v0), nxt(v0),
            pl.BlockSpec((hps,) + bias_blocks.shape[1:], lambda h, rb: (h, 0, 0, 0)),
            pl.BlockSpec((1, wb), lambda h, rb: (0, h)),
        ],
        out_specs=cur(0),
        out_shape=jax.ShapeDtypeStruct((b, s, width), BF16),
        scratch_shapes=[pltpu.VMEM((hps, 2, tq // 2, tq // 2 + 2 * th), F32),
                        pltpu.VMEM((b, hps, tq + 2 * th, 2 * HEAD_DIM), BF16)],
        compiler_params=_cparams(("arbitrary", "arbitrary")),
        name="neighbourhood_attention",
    )(proj, proj, proj, proj, proj, proj, proj, bias_blocks, gain)


def _mem_kv_kernel(m_ref, g_ref, w_ref, kg_ref, k_ref, v_ref, h_scr):
    j = pl.program_id(0)

    @pl.when(j == 0)
    def _():
        h_scr[...] = (_rms(m_ref[...]) * g_ref[...]).astype(BF16)

    acc = jnp.dot(h_scr[...], w_ref[...], preferred_element_type=F32)

    @pl.when(j == 0)
    def _():
        _head_norm_store(k_ref, acc, kg_ref[...], acc.shape[1] // HEAD_DIM)

    @pl.when(j == 1)
    def _():
        v_ref[...] = acc.astype(v_ref.dtype)


def _memory_kv(mem2, gain, w_kv, k_gain, layer):
    n, d = mem2.shape
    width = w_kv.shape[2] // 2
    full = lambda j: (0, 0)
    return pl.pallas_call(
        _mem_kv_kernel,
        grid=(2,),
        in_specs=[
            pl.BlockSpec((n, d), full),
            pl.BlockSpec((1, d), full),
            pl.BlockSpec((None, d, width), lambda j: (layer, 0, j)),
            pl.BlockSpec((1, width), full),
        ],
        out_specs=[pl.BlockSpec((n, width), full), pl.BlockSpec((n, width), full)],
        out_shape=[jax.ShapeDtypeStruct((n, width), BF16), jax.ShapeDtypeStruct((n, width), BF16)],
        scratch_shapes=[pltpu.VMEM((n, d), BF16)],
        compiler_params=_cparams(("arbitrary",)),
        name="memory_kv",
    )(mem2, gain, w_kv, k_gain)


def _mem_attn_kernel(q_ref, k_ref, v_ref, g_ref, o_ref, vx_scr):
    n_heads = q_ref.shape[2] // HEAD_DIM
    n_mem = v_ref.shape[1]
    for h in range(n_heads):
        vx_scr[h, :, :HEAD_DIM] = v_ref[0, :, h * HEAD_DIM:(h + 1) * HEAD_DIM]
        vx_scr[h, :, HEAD_DIM:] = jnp.ones((n_mem, HEAD_DIM), BF16)
    for h in range(n_heads):
        sl = slice(h * HEAD_DIM, (h + 1) * HEAD_DIM)
        s = lax.dot_general(q_ref[0, :, sl], k_ref[0, :, sl], (((1,), (1,)), ((), ())),
                            preferred_element_type=F32)
        m = jnp.max(s, axis=-1, keepdims=True)
        pb = jnp.exp2((s - m).astype(BF16))
        ol = jnp.dot(pb, vx_scr[h], preferred_element_type=F32)
        o = ol[:, :HEAD_DIM] * (1.0 / ol[:, HEAD_DIM:])
        o_ref[0, :, sl] = (_rms(o) * g_ref[:, sl]).astype(o_ref.dtype)


def _memory_attention(proj, q_col, k, v, gain, tm=1024):
    b, s, _ = proj.shape
    m, width = k.shape[1], k.shape[2]
    assert q_col % width == 0
    return pl.pallas_call(
        _mem_attn_kernel,
        grid=(b, s // tm),
        in_specs=[
            pl.BlockSpec((1, tm, width), lambda bi, t: (bi, t, q_col // width)),
            pl.BlockSpec((1, m, width), lambda bi, t: (bi, 0, 0)),
            pl.BlockSpec((1, m, width), lambda bi, t: (bi, 0, 0)),
            pl.BlockSpec((1, width), lambda bi, t: (0, 0)),
        ],
        out_specs=pl.BlockSpec((1, tm, width), lambda bi, t: (bi, t, 0)),
        out_shape=jax.ShapeDtypeStruct((b, s, width), BF16),
        scratch_shapes=[pltpu.VMEM((width // HEAD_DIM, m, 2 * HEAD_DIM), BF16)],
        compiler_params=_cparams(("parallel", "arbitrary")),
        name="memory_attention",
    )(proj, k, v, gain)


def _dft_matrices(seq):
    r1 = DFT_R1
    r2 = seq // r1
    sub = DFT_SUB
    k2 = np.arange(r2, dtype=np.int64)
    s2 = np.arange(r2, dtype=np.int64)
    s1 = np.arange(r1, dtype=np.int64)
    k1 = np.arange(r1, dtype=np.int64)
    th = 2.0 * np.pi * ((k2[:, None] * s2[None, :]) % r2).astype(np.float64) / r2
    fa = np.stack([np.cos(th), -np.sin(th)], axis=0)
    ma = np.einsum("cks,ab->ckasb", fa, np.eye(sub))
    ma = jnp.asarray(ma.reshape(2 * r2 * sub, r2 * sub), BF16)
    tw = 2.0 * np.pi * ((s1[:, None] * k2[None, :]) % seq).astype(np.float64) / seq
    tw = tw.reshape(r1 // sub, sub, r2).transpose(0, 2, 1).reshape(r1 // sub, r2 * sub, 1)
    tw_cos = jnp.broadcast_to(jnp.asarray(np.cos(tw), F32), (r1 // sub, r2 * sub, HEAD_DIM))
    tw_sin = jnp.broadcast_to(jnp.asarray(np.sin(tw), F32), (r1 // sub, r2 * sub, HEAD_DIM))
    ph = 2.0 * np.pi * ((k1[:, None] * s1[None, :]) % r1).astype(np.float64) / r1
    c, s = np.cos(ph), np.sin(ph)
    g = np.stack([np.stack([c, s], 0), np.stack([s, -c], 0)], 0)
    mc = np.einsum("ocks,ab->okacbs", g, np.eye(sub))
    mc = jnp.asarray(mc.reshape(2 * r1 * sub, 2 * sub * r1), BF16)
    cc = np.arange(HEAD_DIM, dtype=np.int64)
    pc = 2.0 * np.pi * ((cc[:, None] * cc[None, :]) % HEAD_DIM).astype(np.float64) / HEAD_DIM
    mch = np.concatenate([np.cos(pc), -np.sin(pc)], axis=0) / np.sqrt(float(seq) * HEAD_DIM)
    return ma, tw_cos, tw_sin, mc, jnp.asarray(mch, BF16)


def _fft_a_kernel(u_ref, ma_ref, twc_ref, tws_ref, t_ref):
    r2, sub, width = u_ref.shape[1], u_ref.shape[2], u_ref.shape[3]
    xb = u_ref[0].reshape(r2 * sub, width).astype(BF16)
    res = jnp.dot(ma_ref[...], xb, preferred_element_type=F32)
    half = r2 * sub
    tw_cos, tw_sin = twc_ref[0], tws_ref[0]
    for g in range(width // HEAD_DIM):
        sl = slice(g * HEAD_DIM, (g + 1) * HEAD_DIM)
        re, im = res[:half, sl], res[half:, sl]
        t_ref[0, 0, :, :, sl] = (re * tw_cos + im * tw_sin).reshape(r2, sub, HEAD_DIM)
        t_ref[0, 1, :, :, sl] = (im * tw_cos - re * tw_sin).reshape(r2, sub, HEAD_DIM)


def _fft_stage_a(u4, ma, tw_cos, tw_sin):
    b, r2, r1, width = u4.shape
    sub = DFT_SUB
    tw_spec = pl.BlockSpec((1, r2 * sub, HEAD_DIM), lambda t, bi: (t, 0, 0))
    return pl.pallas_call(
        _fft_a_kernel,
        grid=(r1 // sub, b),
        in_specs=[
            pl.BlockSpec((1, r2, sub, width), lambda t, bi: (bi, 0, t, 0)),
            pl.BlockSpec(ma.shape, lambda t, bi: (0, 0)),
            tw_spec, tw_spec,
        ],
        out_specs=pl.BlockSpec((1, 2, r2, sub, width), lambda t, bi: (bi, 0, 0, t, 0)),
        out_shape=jax.ShapeDtypeStruct((b, 2, r2, r1, width), F32),
        compiler_params=_cparams(("arbitrary", "arbitrary")),
        name="fourier_stage_a",
    )(u4, ma, tw_cos, tw_sin)


def _fft_c_kernel(t_ref, mc_ref, mch_ref, wf_ref, g_ref, o_ref):
    sub, r1, width = t_ref.shape[2], t_ref.shape[3], t_ref.shape[4]
    tb = t_ref[0].reshape(2 * sub * r1, width).astype(BF16)
    res = jnp.dot(mc_ref[...], tb, preferred_element_type=F32)
    half = r1 * sub
    for g in range(width // HEAD_DIM):
        sl = slice(g * HEAD_DIM, (g + 1) * HEAD_DIM)
        pq = jnp.concatenate([res[:half, sl], res[half:, sl]], axis=-1).astype(BF16)
        y = jnp.dot(pq, mch_ref[...], preferred_element_type=F32)
        z = jnp.dot(y.astype(BF16), wf_ref[g], preferred_element_type=F32)
        z = _rms(z) * g_ref[:, sl]
        o_ref[0, :, :, sl] = z.reshape(r1, sub, HEAD_DIM)


def _fft_stage_c(t5, mc, mch, w_f, gain, layer):
    b, _, r2, r1, width = t5.shape
    sub = DFT_SUB
    groups = width // HEAD_DIM
    return pl.pallas_call(
        _fft_c_kernel,
        grid=(r2 // sub, b),
        in_specs=[
            pl.BlockSpec((1, 2, sub, r1, width), lambda t, bi: (bi, 0, t, 0, 0)),
            pl.BlockSpec(mc.shape, lambda t, bi: (0, 0)),
            pl.BlockSpec(mch.shape, lambda t, bi: (0, 0)),
            pl.BlockSpec((None, groups, HEAD_DIM, HEAD_DIM), lambda t, bi: (layer, 0, 0, 0)),
            pl.BlockSpec((1, width), lambda t, bi: (0, 0)),
        ],
        out_specs=pl.BlockSpec((1, r1, sub, width), lambda t, bi: (bi, 0, t, 0)),
        out_shape=jax.ShapeDtypeStruct((b, r1, r2, width), F32),
        compiler_params=_cparams(("arbitrary", "arbitrary")),
        name="fourier_stage_c",
    )(t5, mc, mch, w_f, gain)


def _outproj_kernel(x_ref, ya_ref, yf_ref, ym_ref, w_ref, o_ref):
    na = ya_ref.shape[1]
    nf = yf_ref.shape[1]
    acc = x_ref[...]
    acc = acc + jnp.dot(ya_ref[...], w_ref[:na, :], preferred_element_type=F32)
    acc = acc + jnp.dot(yf_ref[...].astype(BF16), w_ref[na:na + nf, :], preferred_element_type=F32)
    acc = acc + jnp.dot(ym_ref[...], w_ref[na + nf:, :], preferred_element_type=F32)
    o_ref[...] = acc


def _out_projection(x2, y_na, y_f, y_m, w_out, layer, tm=512):
    n, d = x2.shape
    row = lambda i: (i, 0)
    return pl.pallas_call(
        _outproj_kernel,
        grid=(n // tm,),
        in_specs=[
            pl.BlockSpec((tm, d), row),
            pl.BlockSpec((tm, y_na.shape[1]), row),
            pl.BlockSpec((tm, y_f.shape[1]), row),
            pl.BlockSpec((tm, y_m.shape[1]), row),
            pl.BlockSpec((None,) + w_out.shape[1:], lambda i: (layer, 0, 0)),
        ],
        out_specs=pl.BlockSpec((tm, d), row),
        out_shape=jax.ShapeDtypeStruct((n, d), F32),
        compiler_params=_cparams(("parallel",)),
        name="out_projection",
    )(x2, y_na, y_f, y_m, w_out)


def _ffn_kernel(x_ref, g_ref, wg_ref, wu_ref, wd_ref, o_ref, h_scr):
    f = pl.program_id(1)

    @pl.when(f == 0)
    def _():
        x = x_ref[...]
        h_scr[...] = (_rms(x) * g_ref[...]).astype(BF16)
        o_ref[...] = x

    h = h_scr[...]
    gate = jnp.dot(h, wg_ref[...], preferred_element_type=F32)
    up = jnp.dot(h, wu_ref[...], preferred_element_type=F32)
    act = (gate * jax.nn.sigmoid(gate) * up).astype(BF16)
    o_ref[...] += jnp.dot(act, wd_ref[...], preferred_element_type=F32)


def _ffn(x2, gain, w_gate, w_up, w_down, layer, tm=1024, tf=256):
    n, d = x2.shape
    d_ff = w_gate.shape[2]
    row = lambda i, f: (i, 0)
    return pl.pallas_call(
        _ffn_kernel,
        grid=(n // tm, d_ff // tf),
        in_specs=[
            pl.BlockSpec((tm, d), row),
            pl.BlockSpec((1, d), lambda i, f: (0, 0)),
            pl.BlockSpec((None, d, tf), lambda i, f: (layer, 0, f)),
            pl.BlockSpec((None, d, tf), lambda i, f: (layer, 0, f)),
            pl.BlockSpec((None, tf, d), lambda i, f: (layer, f, 0)),
        ],
        out_specs=pl.BlockSpec((tm, d), row),
        out_shape=jax.ShapeDtypeStruct((n, d), F32),
        scratch_shapes=[pltpu.VMEM((tm, d), BF16)],
        compiler_params=_cparams(("parallel", "arbitrary"), vmem=56 * MIB),
        name="swiglu_ffn",
    )(x2, gain, w_gate, w_up, w_down)


def kernel(x, mem, attn_norm, w_in, na_q_norm, na_k_norm, na_rpb, w_fourier, mem_norm, w_mem_kv,
           mem_q_norm, mem_k_norm, out_norm, w_out, ffn_norm, w_gate, w_up, w_down):
    b, s, d = x.shape
    m = mem.shape[1]
    depth = w_in.shape[0]
    n_mem_heads = w_mem_kv.shape[2] // 2 // HEAD_DIM
    fourier_width = w_fourier.shape[1] * HEAD_DIM
    mem_width = n_mem_heads * HEAD_DIM
    na_width = (w_in.shape[2] - fourier_width - mem_width) // 3
    na_heads = na_width // HEAD_DIM
    q_scale = HEAD_DIM ** -0.5 * LOG2E

    ma, tw_cos, tw_sin, mc, mch = _dft_matrices(s)
    x2 = x.reshape(b * s, d)
    mem2 = mem.reshape(b * m, d)
    w_in_b, w_kv_b, w_f_b, w_out_b = (w.astype(BF16) for w in (w_in, w_mem_kv, w_fourier, w_out))
    w_gate_b, w_up_b, w_down_b = (w.astype(BF16) for w in (w_gate, w_up, w_down))
    norm_flag = jnp.asarray(np.concatenate([
        np.ones(2 * na_width), np.zeros(na_width + fourier_width), np.ones(mem_width)]), F32).reshape(1, -1)

    for l in range(depth):
        head_gain = jnp.concatenate([
            jnp.tile(na_q_norm[l] * q_scale, na_heads),
            jnp.tile(na_k_norm[l], na_heads),
            jnp.ones((na_width + fourier_width,), F32),
            jnp.tile(mem_q_norm[l] * q_scale, n_mem_heads),
        ]).reshape(1, -1)
        proj, u_f = _in_projection(x2, attn_norm[l].reshape(1, d), w_in_b, head_gain, norm_flag, l,
                                   u_col=3 * na_width, u_width=fourier_width)
        proj = proj.reshape(b, s, -1)

        gain_out = out_norm[l].reshape(1, -1)
        y_na = _neighbourhood_attention(proj, _na_bias_blocks(na_rpb[l]), gain_out[:, :na_width], na_heads)

        k_m, v_m = _memory_kv(mem2, mem_norm[l].reshape(1, d), w_kv_b,
                              jnp.tile(mem_k_norm[l], n_mem_heads).reshape(1, -1), l)
        y_m = _memory_attention(proj, 3 * na_width + fourier_width, k_m.reshape(b, m, mem_width),
                                v_m.reshape(b, m, mem_width), gain_out[:, na_width + fourier_width:])

        t5 = _fft_stage_a(u_f.reshape(b, s // DFT_R1, DFT_R1, fourier_width), ma, tw_cos, tw_sin)
        y_f = _fft_stage_c(t5, mc, mch, w_f_b, gain_out[:, na_width:na_width + fourier_width], l)

        x2 = _out_projection(x2, y_na.reshape(b * s, na_width), y_f.reshape(b * s, fourier_width),
                             y_m.reshape(b * s, mem_width), w_out_b, l)
        x2 = _ffn(x2, ffn_norm[l].reshape(1, d), w_gate_b, w_up_b, w_down_b, l)
    return x2.reshape(b, s, d)
```

```python
import functools

import numpy as np
import jax
import jax.numpy as jnp
from jax import lax
from jax.experimental import pallas as pl
from jax.experimental.pallas import tpu as pltpu

F32 = jnp.float32
BF16 = jnp.bfloat16

EPS = 1e-6
HEAD_DIM = 128
MXU_N = 256
GRID_W = 64
NA_KH = 8
NA_KW = 16
NA_ROWS_PER_BLOCK = 8
NA_HALO_ROWS = 4
NA_HEADS_PER_STEP = 2
MASK_VALUE = -1e30
NA_INVALID_ROW = 2 * NA_KH - 1
LOG2E = 1.4426950408889634

DFT_R1 = 64
DFT_SUB = 8

MIB = 1024 * 1024
VMEM_LIMIT = 48 * MIB


def _cparams(sem, vmem=VMEM_LIMIT):
    return pltpu.CompilerParams(dimension_semantics=sem, vmem_limit_bytes=vmem)


def _rms(x, eps=EPS):
    return x * lax.rsqrt(jnp.mean(x * x, axis=-1, keepdims=True) + eps)


def _head_norm_store(dst_ref, acc, gain, n_heads, col0=0):
    for h in range(n_heads):
        sl = slice(col0 + h * HEAD_DIM, col0 + (h + 1) * HEAD_DIM)
        dl = slice(h * HEAD_DIM, (h + 1) * HEAD_DIM)
        dst_ref[:, dl] = (_rms(acc[:, sl]) * gain[:, sl]).astype(dst_ref.dtype)


def _inproj_kernel(x_ref, g_ref, w_ref, hg_ref, p_ref, u_ref, *, normed_heads, u_col, row_chunks):
    rows = x_ref.shape[0] // row_chunks
    u_width = u_ref.shape[1]
    for r in range(row_chunks):
        rs = slice(r * rows, (r + 1) * rows)
        h = (_rms(x_ref[rs, :]) * g_ref[...]).astype(BF16)
        for c in range(w_ref.shape[1] // MXU_N):
            acc = jnp.dot(h, w_ref[:, c * MXU_N:(c + 1) * MXU_N], preferred_element_type=F32)
            for hh in range(MXU_N // HEAD_DIM):
                blk = acc[:, hh * HEAD_DIM:(hh + 1) * HEAD_DIM]
                head = c * (MXU_N // HEAD_DIM) + hh
                sl = slice(head * HEAD_DIM, (head + 1) * HEAD_DIM)
                if head in normed_heads:
                    p_ref[rs, sl] = (_rms(blk) * hg_ref[:, sl]).astype(p_ref.dtype)
                else:
                    p_ref[rs, sl] = blk.astype(p_ref.dtype)
                if u_col <= sl.start < u_col + u_width:
                    u_ref[rs, sl.start - u_col:sl.stop - u_col] = blk


def _in_projection(x2, gain, w_in, head_gain, normed_heads, layer, u_col, u_width, tm=512, row_chunks=2):
    n, d = x2.shape
    d_in = w_in.shape[2]
    return pl.pallas_call(
        functools.partial(_inproj_kernel, normed_heads=normed_heads, u_col=u_col, row_chunks=row_chunks),
        grid=(n // tm,),
        in_specs=[
            pl.BlockSpec((tm, d), lambda i: (i, 0)),
            pl.BlockSpec((1, d), lambda i: (0, 0)),
            pl.BlockSpec((None, d, d_in), lambda i: (layer, 0, 0)),
            pl.BlockSpec((1, d_in), lambda i: (0, 0)),
        ],
        out_specs=[
            pl.BlockSpec((tm, d_in), lambda i: (i, 0)),
            pl.BlockSpec((tm, u_width), lambda i: (i, 0)),
        ],
        out_shape=[
            jax.ShapeDtypeStruct((n, d_in), BF16),
            jax.ShapeDtypeStruct((n, u_width), F32),
        ],
        compiler_params=_cparams(("parallel",)),
        name="in_projection",
    )(x2, gain, w_in, head_gain)


def _na_build_bias(variant, e_ref, bias_scr, hh):
    rq, halo = NA_ROWS_PER_BLOCK, NA_HALO_ROWS
    half_rows = rq // 2
    n_pairs = (half_rows + NA_KH) // 2
    lane = lax.broadcasted_iota(jnp.int32, (GRID_W, 2 * GRID_W), 1)
    for il in range(rq):
        jl0 = jnp.where(variant == 0, max(il, halo), jnp.where(variant == 2, min(il, halo), il))
        half = il // half_rows
        rows = slice((il % half_rows) * GRID_W, (il % half_rows + 1) * GRID_W)
        for pair in range(n_pairs):
            blocks = []
            for jl in (half * halo + 2 * pair, half * halo + 2 * pair + 1):
                inside = jnp.logical_and(jl >= jl0, jl < jl0 + NA_KH)
                idx = jnp.where(inside, jl - il - halo + NA_KH - 1, NA_INVALID_ROW)
                blocks.append(e_ref[hh, idx])
            bias_scr[hh, half, rows, pair * 2 * GRID_W:(pair + 1) * 2 * GRID_W] = (
                jnp.where(lane < GRID_W, blocks[0], blocks[1]))


def _na_kernel(q_ref, kp_ref, kc_ref, kn_ref, vp_ref, vc_ref, vn_ref, e_ref, g_ref, o_ref,
               bias_scr, vx_scr):
    n_batch = q_ref.shape[0]
    n_heads = q_ref.shape[2] // HEAD_DIM
    rb = pl.program_id(1)
    last = pl.num_programs(1) - 1
    half_tokens = q_ref.shape[1] // 2
    halo_tokens = kp_ref.shape[1]
    win_tokens = half_tokens + 2 * halo_tokens
    contract_last = (((1,), (1,)), ((), ()))

    @pl.when(jnp.logical_or(rb <= 1, rb == last))
    def _():
        variant = jnp.where(rb == 0, 0, jnp.where(rb == last, 2, 1))
        for hh in range(n_heads):
            _na_build_bias(variant, e_ref, bias_scr, hh)

    for hh in range(n_heads):
        hs = slice(hh * HEAD_DIM, (hh + 1) * HEAD_DIM)
        for b in range(n_batch):
            row0 = 0
            for v_ref in (vp_ref, vc_ref, vn_ref):
                vx_scr[b, hh, row0:row0 + v_ref.shape[1], :HEAD_DIM] = v_ref[b, :, hs]
                row0 += v_ref.shape[1]
            vx_scr[b, hh, :, HEAD_DIM:] = jnp.ones((row0, HEAD_DIM), BF16)

    k_segments = ((kp_ref, kc_ref), (kc_ref, kn_ref))
    for hh in range(n_heads):
        hs = slice(hh * HEAD_DIM, (hh + 1) * HEAD_DIM)
        for b in range(n_batch):
            for half in range(2):
                rows = slice(half * half_tokens, (half + 1) * half_tokens)
                keys = slice(half * halo_tokens, half * halo_tokens + win_tokens)
                q = q_ref[b, rows, hs]
                s = jnp.concatenate(
                    [lax.dot_general(q, k_ref[b, :, hs], contract_last, preferred_element_type=F32)
                     for k_ref in k_segments[half]], axis=-1)
                s = s + bias_scr[hh, half]
                m = jnp.max(s, axis=-1, keepdims=True)
                pb = jnp.exp2((s - m).astype(BF16))
                ol = jnp.dot(pb, vx_scr[b, hh, keys, :], preferred_element_type=F32)
                o = ol[:, :HEAD_DIM] * (1.0 / ol[:, HEAD_DIM:])
                o_ref[b, rows, hs] = (_rms(o) * g_ref[:, hs]).astype(o_ref.dtype)


def _na_bias_blocks(rpb):
    n_heads = rpb.shape[0]
    kw = min(NA_KW, GRID_W)
    col = np.arange(GRID_W)
    col_start = np.clip(col - kw // 2, 0, GRID_W - kw)
    col_ok = (col[None, :] >= col_start[:, None]) & (col[None, :] < col_start[:, None] + kw)
    col_idx = np.clip(col[None, :] - col[:, None] + NA_KW - 1, 0, 2 * NA_KW - 2)
    onehot = jnp.asarray(col_idx[None] == np.arange(2 * NA_KW - 1)[:, None, None], F32)
    picked = jnp.einsum("hde,eqk->hdqk", rpb, onehot, precision=lax.Precision.HIGHEST)
    toe = jnp.where(col_ok[None, None], picked * LOG2E, MASK_VALUE)
    toe = jnp.concatenate([toe, jnp.full((n_heads, 1, GRID_W, GRID_W), MASK_VALUE, F32)], axis=1)
    return jnp.concatenate([toe, toe], axis=-1)


def _neighbourhood_attention(proj, bias_blocks, gain, n_heads):
    b, s, _ = proj.shape
    width = n_heads * HEAD_DIM
    tq = NA_ROWS_PER_BLOCK * GRID_W
    th = NA_HALO_ROWS * GRID_W
    n_blocks = s // tq
    n_halo_blocks = s // th
    per = tq // th
    assert n_blocks >= 3 and NA_KH == 2 * NA_HALO_ROWS == NA_ROWS_PER_BLOCK

    hps = NA_HEADS_PER_STEP
    wb = hps * HEAD_DIM
    assert n_heads % hps == 0

    def cur(col0):
        return pl.BlockSpec((b, tq, wb), lambda h, rb: (0, rb, col0 + h))

    def prev(col0):
        return pl.BlockSpec((b, th, wb), lambda h, rb: (0, jnp.maximum(per * rb - 1, 0), col0 + h))

    def nxt(col0):
        return pl.BlockSpec((b, th, wb),
                            lambda h, rb: (0, jnp.minimum(per * rb + per, n_halo_blocks - 1), col0 + h))

    k0, v0 = n_heads // hps, 2 * n_heads // hps
    return pl.pallas_call(
        _na_kernel,
        grid=(n_heads // hps, n_blocks),
        in_specs=[
            cur(0), prev(k0), cur(k0), nxt(k0), prev(v0), cur(v0), nxt(v0),
            pl.BlockSpec((hps,) + bias_blocks.shape[1:], lambda h, rb: (h, 0, 0, 0)),
            pl.BlockSpec((1, wb), lambda h, rb: (0, h)),
        ],
        out_specs=cur(0),
        out_shape=jax.ShapeDtypeStruct((b, s, width), BF16),
        scratch_shapes=[pltpu.VMEM((hps, 2, tq // 2, tq // 2 + 2 * th), F32),
                        pltpu.VMEM((b, hps, tq + 2 * th, 2 * HEAD_DIM), BF16)],
        compiler_params=_cparams(("arbitrary", "arbitrary")),
        name="neighbourhood_attention",
    )(proj, proj, proj, proj, proj, proj, proj, bias_blocks, gain)


def _mem_kv_kernel(m_ref, g_ref, w_ref, kg_ref, k_ref, v_ref, h_scr):
    j = pl.program_id(0)

    @pl.when(j == 0)
    def _():
        h_scr[...] = (_rms(m_ref[...]) * g_ref[...]).astype(BF16)

    acc = jnp.dot(h_scr[...], w_ref[...], preferred_element_type=F32)

    @pl.when(j == 0)
    def _():
        _head_norm_store(k_ref, acc, kg_ref[...], acc.shape[1] // HEAD_DIM)

    @pl.when(j == 1)
    def _():
        v_ref[...] = acc.astype(v_ref.dtype)


def _memory_kv(mem2, gain, w_kv, k_gain, layer):
    n, d = mem2.shape
    width = w_kv.shape[2] // 2
    full = lambda j: (0, 0)
    return pl.pallas_call(
        _mem_kv_kernel,
        grid=(2,),
        in_specs=[
            pl.BlockSpec((n, d), full),
            pl.BlockSpec((1, d), full),
            pl.BlockSpec((None, d, width), lambda j: (layer, 0, j)),
            pl.BlockSpec((1, width), full),
        ],
        out_specs=[pl.BlockSpec((n, width), full), pl.BlockSpec((n, width), full)],
        out_shape=[jax.ShapeDtypeStruct((n, width), BF16), jax.ShapeDtypeStruct((n, width), BF16)],
        scratch_shapes=[pltpu.VMEM((n, d), BF16)],
        compiler_params=_cparams(("arbitrary",)),
        name="memory_kv",
    )(mem2, gain, w_kv, k_gain)


def _mem_attn_kernel(q_ref, k_ref, v_ref, g_ref, o_ref, vx_scr):
    n_heads = q_ref.shape[2] // HEAD_DIM
    n_mem = v_ref.shape[1]
    for h in range(n_heads):
        vx_scr[h, :, :HEAD_DIM] = v_ref[0, :, h * HEAD_DIM:(h + 1) * HEAD_DIM]
        vx_scr[h, :, HEAD_DIM:] = jnp.ones((n_mem, HEAD_DIM), BF16)
    for h in range(n_heads):
        sl = slice(h * HEAD_DIM, (h + 1) * HEAD_DIM)
        s = lax.dot_general(q_ref[0, :, sl], k_ref[0, :, sl], (((1,), (1,)), ((), ())),
                            preferred_element_type=F32)
        m = jnp.max(s, axis=-1, keepdims=True)
        pb = jnp.exp2((s - m).astype(BF16))
        ol = jnp.dot(pb, vx_scr[h], preferred_element_type=F32)
        o = ol[:, :HEAD_DIM] * (1.0 / ol[:, HEAD_DIM:])
        o_ref[0, :, sl] = (_rms(o) * g_ref[:, sl]).astype(o_ref.dtype)


def _memory_attention(proj, q_col, k, v, gain, tm=1024):
    b, s, _ = proj.shape
    m, width = k.shape[1], k.shape[2]
    assert q_col % width == 0
    return pl.pallas_call(
        _mem_attn_kernel,
        grid=(b, s // tm),
        in_specs=[
            pl.BlockSpec((1, tm, width), lambda bi, t: (bi, t, q_col // width)),
            pl.BlockSpec((1, m, width), lambda bi, t: (bi, 0, 0)),
            pl.BlockSpec((1, m, width), lambda bi, t: (bi, 0, 0)),
            pl.BlockSpec((1, width), lambda bi, t: (0, 0)),
        ],
        out_specs=pl.BlockSpec((1, tm, width), lambda bi, t: (bi, t, 0)),
        out_shape=jax.ShapeDtypeStruct((b, s, width), BF16),
        scratch_shapes=[pltpu.VMEM((width // HEAD_DIM, m, 2 * HEAD_DIM), BF16)],
        compiler_params=_cparams(("parallel", "arbitrary")),
        name="memory_attention",
    )(proj, k, v, gain)


def _dft_matrices(seq):
    r1 = DFT_R1
    r2 = seq // r1
    sub = DFT_SUB
    k2 = np.arange(r2, dtype=np.int64)
    s2 = np.arange(r2, dtype=np.int64)
    s1 = np.arange(r1, dtype=np.int64)
    k1 = np.arange(r1, dtype=np.int64)
    th = 2.0 * np.pi * ((k2[:, None] * s2[None, :]) % r2).astype(np.float64) / r2
    fa = np.stack([np.cos(th), -np.sin(th)], axis=0)
    ma = np.einsum("cks,ab->ckasb", fa, np.eye(sub))
    ma = jnp.asarray(ma.reshape(2 * r2 * sub, r2 * sub), BF16)
    tw = 2.0 * np.pi * ((s1[:, None] * k2[None, :]) % seq).astype(np.float64) / seq
    tw = tw.reshape(r1 // sub, sub, r2).transpose(0, 2, 1).reshape(r1 // sub, r2 * sub, 1)
    tw_cos = jnp.broadcast_to(jnp.asarray(np.cos(tw), F32), (r1 // sub, r2 * sub, HEAD_DIM))
    tw_sin = jnp.broadcast_to(jnp.asarray(np.sin(tw), F32), (r1 // sub, r2 * sub, HEAD_DIM))
    ph = 2.0 * np.pi * ((k1[:, None] * s1[None, :]) % r1).astype(np.float64) / r1
    c, s = np.cos(ph), np.sin(ph)
    g = np.stack([np.stack([c, s], 0), np.stack([s, -c], 0)], 0)
    mc = np.einsum("ocks,ab->okacbs", g, np.eye(sub))
    mc = jnp.asarray(mc.reshape(2 * r1 * sub, 2 * sub * r1), BF16)
    cc = np.arange(HEAD_DIM, dtype=np.int64)
    pc = 2.0 * np.pi * ((cc[:, None] * cc[None, :]) % HEAD_DIM).astype(np.float64) / HEAD_DIM
    mch = np.concatenate([np.cos(pc), -np.sin(pc)], axis=0) / np.sqrt(float(seq) * HEAD_DIM)
    return ma, tw_cos, tw_sin, mc, jnp.asarray(mch, BF16)


def _fft_a_kernel(u_ref, ma_ref, twc_ref, tws_ref, t_ref):
    r2, sub, width = u_ref.shape[1], u_ref.shape[2], u_ref.shape[3]
    xb = u_ref[0].reshape(r2 * sub, width).astype(BF16)
    res = jnp.dot(ma_ref[...], xb, preferred_element_type=F32)
    half = r2 * sub
    tw_cos, tw_sin = twc_ref[0], tws_ref[0]
    for g in range(width // HEAD_DIM):
        sl = slice(g * HEAD_DIM, (g + 1) * HEAD_DIM)
        re, im = res[:half, sl], res[half:, sl]
        t_ref[0, 0, :, :, sl] = (re * tw_cos + im * tw_sin).reshape(r2, sub, HEAD_DIM)
        t_ref[0, 1, :, :, sl] = (im * tw_cos - re * tw_sin).reshape(r2, sub, HEAD_DIM)


def _fft_stage_a(u4, ma, tw_cos, tw_sin):
    b, r2, r1, width = u4.shape
    sub = DFT_SUB
    tw_spec = pl.BlockSpec((1, r2 * sub, HEAD_DIM), lambda t, bi: (t, 0, 0))
    return pl.pallas_call(
        _fft_a_kernel,
        grid=(r1 // sub, b),
        in_specs=[
            pl.BlockSpec((1, r2, sub, width), lambda t, bi: (bi, 0, t, 0)),
            pl.BlockSpec(ma.shape, lambda t, bi: (0, 0)),
            tw_spec, tw_spec,
        ],
        out_specs=pl.BlockSpec((1, 2, r2, sub, width), lambda t, bi: (bi, 0, 0, t, 0)),
        out_shape=jax.ShapeDtypeStruct((b, 2, r2, r1, width), F32),
        compiler_params=_cparams(("arbitrary", "arbitrary")),
        name="fourier_stage_a",
    )(u4, ma, tw_cos, tw_sin)


def _fft_c_kernel(t_ref, mc_ref, mch_ref, wf_ref, g_ref, o_ref):
    sub, r1, width = t_ref.shape[2], t_ref.shape[3], t_ref.shape[4]
    tb = t_ref[0].reshape(2 * sub * r1, width).astype(BF16)
    res = jnp.dot(mc_ref[...], tb, preferred_element_type=F32)
    half = r1 * sub
    for g in range(width // HEAD_DIM):
        sl = slice(g * HEAD_DIM, (g + 1) * HEAD_DIM)
        pq = jnp.concatenate([res[:half, sl], res[half:, sl]], axis=-1).astype(BF16)
        y = jnp.dot(pq, mch_ref[...], preferred_element_type=F32)
        z = jnp.dot(y.astype(BF16), wf_ref[g], preferred_element_type=F32)
        z = _rms(z) * g_ref[:, sl]
        o_ref[0, :, :, sl] = z.reshape(r1, sub, HEAD_DIM)


def _fft_stage_c(t5, mc, mch, w_f, gain, layer):
    b, _, r2, r1, width = t5.shape
    sub = DFT_SUB
    groups = width // HEAD_DIM
    return pl.pallas_call(
        _fft_c_kernel,
        grid=(r2 // sub, b),
        in_specs=[
            pl.BlockSpec((1, 2, sub, r1, width), lambda t, bi: (bi, 0, t, 0, 0)),
            pl.BlockSpec(mc.shape, lambda t, bi: (0, 0)),
            pl.BlockSpec(mch.shape, lambda t, bi: (0, 0)),
            pl.BlockSpec((None, groups, HEAD_DIM, HEAD_DIM), lambda t, bi: (layer, 0, 0, 0)),
            pl.BlockSpec((1, width), lambda t, bi: (0, 0)),
        ],
        out_specs=pl.BlockSpec((1, r1, sub, width), lambda t, bi: (bi, 0, t, 0)),
        out_shape=jax.ShapeDtypeStruct((b, r1, r2, width), F32),
        compiler_params=_cparams(("arbitrary", "arbitrary")),
        name="fourier_stage_c",
    )(t5, mc, mch, w_f, gain)


def _outproj_kernel(x_ref, ya_ref, yf_ref, ym_ref, w_ref, o_ref):
    na = ya_ref.shape[1]
    nf = yf_ref.shape[1]
    acc = x_ref[...]
    acc = acc + jnp.dot(ya_ref[...], w_ref[:na, :], preferred_element_type=F32)
    acc = acc + jnp.dot(yf_ref[...].astype(BF16), w_ref[na:na + nf, :], preferred_element_type=F32)
    acc = acc + jnp.dot(ym_ref[...], w_ref[na + nf:, :], preferred_element_type=F32)
    o_ref[...] = acc


def _out_projection(x2, y_na, y_f, y_m, w_out, layer, tm=512):
    n, d = x2.shape
    row = lambda i: (i, 0)
    return pl.pallas_call(
        _outproj_kernel,
        grid=(n // tm,),
        in_specs=[
            pl.BlockSpec((tm, d), row),
            pl.BlockSpec((tm, y_na.shape[1]), row),
            pl.BlockSpec((tm, y_f.shape[1]), row),
            pl.BlockSpec((tm, y_m.shape[1]), row),
            pl.BlockSpec((None,) + w_out.shape[1:], lambda i: (layer, 0, 0)),
        ],
        out_specs=pl.BlockSpec((tm, d), row),
        out_shape=jax.ShapeDtypeStruct((n, d), F32),
        compiler_params=_cparams(("parallel",)),
        name="out_projection",
    )(x2, y_na, y_f, y_m, w_out)


def _ffn_kernel(x_ref, g_ref, wg_ref, wu_ref, wd_ref, o_ref, h_scr):
    f = pl.program_id(1)

    @pl.when(f == 0)
    def _():
        x = x_ref[...]
        h_scr[...] = (_rms(x) * g_ref[...]).astype(BF16)
        o_ref[...] = x

    h = h_scr[...]
    gate = jnp.dot(h, wg_ref[...], preferred_element_type=F32)
    up = jnp.dot(h, wu_ref[...], preferred_element_type=F32)
    act = (gate * jax.nn.sigmoid(gate) * up).astype(BF16)
    o_ref[...] += jnp.dot(act, wd_ref[...], preferred_element_type=F32)


def _ffn(x2, gain, w_gate, w_up, w_down, layer, tm=1024, tf=512):
    n, d = x2.shape
    d_ff = w_gate.shape[2]
    row = lambda i, f: (i, 0)
    return pl.pallas_call(
        _ffn_kernel,
        grid=(n // tm, d_ff // tf),
        in_specs=[
            pl.BlockSpec((tm, d), row),
            pl.BlockSpec((1, d), lambda i, f: (0, 0)),
            pl.BlockSpec((None, d, tf), lambda i, f: (layer, 0, f)),
            pl.BlockSpec((None, d, tf), lambda i, f: (layer, 0, f)),
            pl.BlockSpec((None, tf, d), lambda i, f: (layer, f, 0)),
        ],
        out_specs=pl.BlockSpec((tm, d), row),
        out_shape=jax.ShapeDtypeStruct((n, d), F32),
        scratch_shapes=[pltpu.VMEM((tm, d), BF16)],
        compiler_params=_cparams(("parallel", "arbitrary"), vmem=56 * MIB),
        name="swiglu_ffn",
    )(x2, gain, w_gate, w_up, w_down)


def kernel(x, mem, attn_norm, w_in, na_q_norm, na_k_norm, na_rpb, w_fourier, mem_norm, w_mem_kv,
           mem_q_norm, mem_k_norm, out_norm, w_out, ffn_norm, w_gate, w_up, w_down):
    b, s, d = x.shape
    m = mem.shape[1]
    depth = w_in.shape[0]
    n_mem_heads = w_mem_kv.shape[2] // 2 // HEAD_DIM
    fourier_width = w_fourier.shape[1] * HEAD_DIM
    mem_width = n_mem_heads * HEAD_DIM
    na_width = (w_in.shape[2] - fourier_width - mem_width) // 3
    na_heads = na_width // HEAD_DIM
    q_scale = HEAD_DIM ** -0.5 * LOG2E

    ma, tw_cos, tw_sin, mc, mch = _dft_matrices(s)
    x2 = x.reshape(b * s, d)
    mem2 = mem.reshape(b * m, d)
    w_in_b, w_kv_b, w_f_b, w_out_b = (w.astype(BF16) for w in (w_in, w_mem_kv, w_fourier, w_out))
    w_gate_b, w_up_b, w_down_b = (w.astype(BF16) for w in (w_gate, w_up, w_down))
    first_mem_head = (3 * na_width + fourier_width) // HEAD_DIM
    normed_heads = frozenset(range(2 * na_heads)) | frozenset(
        range(first_mem_head, first_mem_head + n_mem_heads))

    for l in range(depth):
        head_gain = jnp.concatenate([
            jnp.tile(na_q_norm[l] * q_scale, na_heads),
            jnp.tile(na_k_norm[l], na_heads),
            jnp.ones((na_width + fourier_width,), F32),
            jnp.tile(mem_q_norm[l] * q_scale, n_mem_heads),
        ]).reshape(1, -1)
        proj, u_f = _in_projection(x2, attn_norm[l].reshape(1, d), w_in_b, head_gain, normed_heads, l,
                                   u_col=3 * na_width, u_width=fourier_width)
        proj = proj.reshape(b, s, -1)

        gain_out = out_norm[l].reshape(1, -1)
        y_na = _neighbourhood_attention(proj, _na_bias_blocks(na_rpb[l]), gain_out[:, :na_width], na_heads)

        k_m, v_m = _memory_kv(mem2, mem_norm[l].reshape(1, d), w_kv_b,
                              jnp.tile(mem_k_norm[l], n_mem_heads).reshape(1, -1), l)
        y_m = _memory_attention(proj, 3 * na_width + fourier_width, k_m.reshape(b, m, mem_width),
                                v_m.reshape(b, m, mem_width), gain_out[:, na_width + fourier_width:])

        t5 = _fft_stage_a(u_f.reshape(b, s // DFT_R1, DFT_R1, fourier_width), ma, tw_cos, tw_sin)
        y_f = _fft_stage_c(t5, mc, mch, w_f_b, gain_out[:, na_width:na_width + fourier_width], l)

        x2 = _out_projection(x2, y_na.reshape(b * s, na_width), y_f.reshape(b * s, fourier_width),
                             y_m.reshape(b * s, mem_width), w_out_b, l)
        x2 = _ffn(x2, ffn_norm[l].reshape(1, d), w_gate_b, w_up_b, w_down_b, l)
    return x2.reshape(b, s, d)
```

```python
import functools

import numpy as np
import jax
import jax.numpy as jnp
from jax import lax
from jax.experimental import pallas as pl
from jax.experimental.pallas import tpu as pltpu

F32 = jnp.float32
BF16 = jnp.bfloat16

EPS = 1e-6
HEAD_DIM = 128
MXU_N = 256
GRID_W = 64
NA_KH = 8
NA_KW = 16
NA_ROWS_PER_BLOCK = 8
NA_HALO_ROWS = 4
NA_HEADS_PER_STEP = 2
MASK_VALUE = -1e30
NA_INVALID_ROW = 2 * NA_KH - 1
LOG2E = 1.4426950408889634

DFT_R1 = 64
DFT_SUB = 8

MIB = 1024 * 1024
VMEM_LIMIT = 48 * MIB


def _cparams(sem, vmem=VMEM_LIMIT):
    return pltpu.CompilerParams(dimension_semantics=sem, vmem_limit_bytes=vmem)


def _rms(x, eps=EPS):
    return x * lax.rsqrt(jnp.mean(x * x, axis=-1, keepdims=True) + eps)


def _head_norm_store(dst_ref, acc, gain, n_heads, col0=0):
    for h in range(n_heads):
        sl = slice(col0 + h * HEAD_DIM, col0 + (h + 1) * HEAD_DIM)
        dl = slice(h * HEAD_DIM, (h + 1) * HEAD_DIM)
        dst_ref[:, dl] = (_rms(acc[:, sl]) * gain[:, sl]).astype(dst_ref.dtype)


def _inproj_kernel(x_ref, g_ref, w_ref, hg_ref, p_ref, u_ref, *, normed_heads, u_col, row_chunks):
    rows = x_ref.shape[0] // row_chunks
    u_width = u_ref.shape[1]
    for r in range(row_chunks):
        rs = slice(r * rows, (r + 1) * rows)
        h = (_rms(x_ref[rs, :]) * g_ref[...]).astype(BF16)
        for c in range(w_ref.shape[1] // MXU_N):
            acc = jnp.dot(h, w_ref[:, c * MXU_N:(c + 1) * MXU_N], preferred_element_type=F32)
            for hh in range(MXU_N // HEAD_DIM):
                blk = acc[:, hh * HEAD_DIM:(hh + 1) * HEAD_DIM]
                head = c * (MXU_N // HEAD_DIM) + hh
                sl = slice(head * HEAD_DIM, (head + 1) * HEAD_DIM)
                if head in normed_heads:
                    p_ref[rs, sl] = (_rms(blk) * hg_ref[:, sl]).astype(p_ref.dtype)
                else:
                    p_ref[rs, sl] = blk.astype(p_ref.dtype)
                if u_col <= sl.start < u_col + u_width:
                    u_ref[rs, sl.start - u_col:sl.stop - u_col] = blk


def _in_projection(x2, gain, w_in, head_gain, normed_heads, layer, u_col, u_width, tm=512, row_chunks=2):
    n, d = x2.shape
    d_in = w_in.shape[2]
    return pl.pallas_call(
        functools.partial(_inproj_kernel, normed_heads=normed_heads, u_col=u_col, row_chunks=row_chunks),
        grid=(n // tm,),
        in_specs=[
            pl.BlockSpec((tm, d), lambda i: (i, 0)),
            pl.BlockSpec((1, d), lambda i: (0, 0)),
            pl.BlockSpec((None, d, d_in), lambda i: (layer, 0, 0)),
            pl.BlockSpec((1, d_in), lambda i: (0, 0)),
        ],
        out_specs=[
            pl.BlockSpec((tm, d_in), lambda i: (i, 0)),
            pl.BlockSpec((tm, u_width), lambda i: (i, 0)),
        ],
        out_shape=[
            jax.ShapeDtypeStruct((n, d_in), BF16),
            jax.ShapeDtypeStruct((n, u_width), F32),
        ],
        compiler_params=_cparams(("parallel",)),
        name="in_projection",
    )(x2, gain, w_in, head_gain)


def _na_build_bias(variant, e_ref, bias_scr, hh):
    rq, halo = NA_ROWS_PER_BLOCK, NA_HALO_ROWS
    half_rows = rq // 2
    n_pairs = (half_rows + NA_KH) // 2
    lane = lax.broadcasted_iota(jnp.int32, (GRID_W, 2 * GRID_W), 1)
    for il in range(rq):
        jl0 = jnp.where(variant == 0, max(il, halo), jnp.where(variant == 2, min(il, halo), il))
        half = il // half_rows
        rows = slice((il % half_rows) * GRID_W, (il % half_rows + 1) * GRID_W)
        for pair in range(n_pairs):
            blocks = []
            for jl in (half * halo + 2 * pair, half * halo + 2 * pair + 1):
                inside = jnp.logical_and(jl >= jl0, jl < jl0 + NA_KH)
                idx = jnp.where(inside, jl - il - halo + NA_KH - 1, NA_INVALID_ROW)
                blocks.append(e_ref[hh, idx])
            bias_scr[hh, half, rows, pair * 2 * GRID_W:(pair + 1) * 2 * GRID_W] = (
                jnp.where(lane < GRID_W, blocks[0], blocks[1]))


def _na_kernel(q_ref, kp_ref, kc_ref, kn_ref, vp_ref, vc_ref, vn_ref, e_ref, g_ref, o_ref,
               bias_scr, vx_scr):
    n_batch = q_ref.shape[0]
    n_heads = q_ref.shape[2] // HEAD_DIM
    rb = pl.program_id(1)
    last = pl.num_programs(1) - 1
    half_tokens = q_ref.shape[1] // 2
    halo_tokens = kp_ref.shape[1]
    win_tokens = half_tokens + 2 * halo_tokens
    contract_last = (((1,), (1,)), ((), ()))

    @pl.when(jnp.logical_or(rb <= 1, rb == last))
    def _():
        variant = jnp.where(rb == 0, 0, jnp.where(rb == last, 2, 1))
        for hh in range(n_heads):
            _na_build_bias(variant, e_ref, bias_scr, hh)
        vx_scr[:, :, :, HEAD_DIM:] = jnp.ones(vx_scr.shape[:3] + (HEAD_DIM,), BF16)

    k_segments = ((kp_ref, kc_ref), (kc_ref, kn_ref))
    for hh in range(n_heads):
        hs = slice(hh * HEAD_DIM, (hh + 1) * HEAD_DIM)
        for b in range(n_batch):
            row0 = 0
            for v_ref in (vp_ref, vc_ref, vn_ref):
                vx_scr[b, hh, row0:row0 + v_ref.shape[1], :HEAD_DIM] = v_ref[b, :, hs]
                row0 += v_ref.shape[1]
            for half in range(2):
                rows = slice(half * half_tokens, (half + 1) * half_tokens)
                keys = slice(half * halo_tokens, half * halo_tokens + win_tokens)
                q = q_ref[b, rows, hs]
                s = jnp.concatenate(
                    [lax.dot_general(q, k_ref[b, :, hs], contract_last, preferred_element_type=F32)
                     for k_ref in k_segments[half]], axis=-1)
                s = s + bias_scr[hh, half]
                m = jnp.max(s, axis=-1, keepdims=True)
                pb = jnp.exp2((s - m).astype(BF16))
                ol = jnp.dot(pb, vx_scr[b, hh, keys, :], preferred_element_type=F32)
                o = ol[:, :HEAD_DIM] * (1.0 / ol[:, HEAD_DIM:])
                o_ref[b, rows, hs] = (_rms(o) * g_ref[:, hs]).astype(o_ref.dtype)


def _na_bias_blocks(rpb):
    n_heads = rpb.shape[0]
    kw = min(NA_KW, GRID_W)
    col = np.arange(GRID_W)
    col_start = np.clip(col - kw // 2, 0, GRID_W - kw)
    col_ok = (col[None, :] >= col_start[:, None]) & (col[None, :] < col_start[:, None] + kw)
    col_idx = np.clip(col[None, :] - col[:, None] + NA_KW - 1, 0, 2 * NA_KW - 2)
    onehot = jnp.asarray(col_idx[None] == np.arange(2 * NA_KW - 1)[:, None, None], F32)
    picked = jnp.einsum("hde,eqk->hdqk", rpb, onehot, precision=lax.Precision.HIGHEST)
    toe = jnp.where(col_ok[None, None], picked * LOG2E, MASK_VALUE)
    toe = jnp.concatenate([toe, jnp.full((n_heads, 1, GRID_W, GRID_W), MASK_VALUE, F32)], axis=1)
    return jnp.concatenate([toe, toe], axis=-1)


def _neighbourhood_attention(proj, bias_blocks, gain, n_heads):
    b, s, _ = proj.shape
    width = n_heads * HEAD_DIM
    tq = NA_ROWS_PER_BLOCK * GRID_W
    th = NA_HALO_ROWS * GRID_W
    n_blocks = s // tq
    n_halo_blocks = s // th
    per = tq // th
    assert n_blocks >= 3 and NA_KH == 2 * NA_HALO_ROWS == NA_ROWS_PER_BLOCK

    hps = NA_HEADS_PER_STEP
    wb = hps * HEAD_DIM
    assert n_heads % hps == 0

    def cur(col0):
        return pl.BlockSpec((b, tq, wb), lambda h, rb: (0, rb, col0 + h))

    def prev(col0):
        return pl.BlockSpec((b, th, wb), lambda h, rb: (0, jnp.maximum(per * rb - 1, 0), col0 + h))

    def nxt(col0):
        return pl.BlockSpec((b, th, wb),
                            lambda h, rb: (0, jnp.minimum(per * rb + per, n_halo_blocks - 1), col0 + h))

    k0, v0 = n_heads // hps, 2 * n_heads // hps
    return pl.pallas_call(
        _na_kernel,
        grid=(n_heads // hps, n_blocks),
        in_specs=[
            cur(0), prev(k0), cur(k0), nxt(k0), prev(v0), cur(v0), nxt(v0),
            pl.BlockSpec((hps,) + bias_blocks.shape[1:], lambda h, rb: (h, 0, 0, 0)),
            pl.BlockSpec((1, wb), lambda h, rb: (0, h)),
        ],
        out_specs=cur(0),
        out_shape=jax.ShapeDtypeStruct((b, s, width), BF16),
        scratch_shapes=[pltpu.VMEM((hps, 2, tq // 2, tq // 2 + 2 * th), F32),
                        pltpu.VMEM((b, hps, tq + 2 * th, 2 * HEAD_DIM), BF16)],
        compiler_params=_cparams(("arbitrary", "arbitrary")),
        name="neighbourhood_attention",
    )(proj, proj, proj, proj, proj, proj, proj, bias_blocks, gain)


def _mem_kv_kernel(m_ref, g_ref, w_ref, kg_ref, k_ref, v_ref, h_scr):
    j = pl.program_id(0)

    @pl.when(j == 0)
    def _():
        h_scr[...] = (_rms(m_ref[...]) * g_ref[...]).astype(BF16)

    acc = jnp.dot(h_scr[...], w_ref[...], preferred_element_type=F32)

    @pl.when(j == 0)
    def _():
        _head_norm_store(k_ref, acc, kg_ref[...], acc.shape[1] // HEAD_DIM)

    @pl.when(j == 1)
    def _():
        v_ref[...] = acc.astype(v_ref.dtype)


def _memory_kv(mem2, gain, w_kv, k_gain, layer):
    n, d = mem2.shape
    width = w_kv.shape[2] // 2
    full = lambda j: (0, 0)
    return pl.pallas_call(
        _mem_kv_kernel,
        grid=(2,),
        in_specs=[
            pl.BlockSpec((n, d), full),
            pl.BlockSpec((1, d), full),
            pl.BlockSpec((None, d, width), lambda j: (layer, 0, j)),
            pl.BlockSpec((1, width), full),
        ],
        out_specs=[pl.BlockSpec((n, width), full), pl.BlockSpec((n, width), full)],
        out_shape=[jax.ShapeDtypeStruct((n, width), BF16), jax.ShapeDtypeStruct((n, width), BF16)],
        scratch_shapes=[pltpu.VMEM((n, d), BF16)],
        compiler_params=_cparams(("arbitrary",)),
        name="memory_kv",
    )(mem2, gain, w_kv, k_gain)


def _mem_attn_kernel(q_ref, k_ref, v_ref, g_ref, o_ref, vx_scr):
    n_heads = q_ref.shape[2] // HEAD_DIM
    n_mem = v_ref.shape[1]
    for h in range(n_heads):
        vx_scr[h, :, :HEAD_DIM] = v_ref[0, :, h * HEAD_DIM:(h + 1) * HEAD_DIM]
        vx_scr[h, :, HEAD_DIM:] = jnp.ones((n_mem, HEAD_DIM), BF16)
    for h in range(n_heads):
        sl = slice(h * HEAD_DIM, (h + 1) * HEAD_DIM)
        s = lax.dot_general(q_ref[0, :, sl], k_ref[0, :, sl], (((1,), (1,)), ((), ())),
                            preferred_element_type=F32)
        m = jnp.max(s, axis=-1, keepdims=True)
        pb = jnp.exp2((s - m).astype(BF16))
        ol = jnp.dot(pb, vx_scr[h], preferred_element_type=F32)
        o = ol[:, :HEAD_DIM] * (1.0 / ol[:, HEAD_DIM:])
        o_ref[0, :, sl] = (_rms(o) * g_ref[:, sl]).astype(o_ref.dtype)


def _memory_attention(proj, q_col, k, v, gain, tm=1024):
    b, s, _ = proj.shape
    m, width = k.shape[1], k.shape[2]
    assert q_col % width == 0
    return pl.pallas_call(
        _mem_attn_kernel,
        grid=(b, s // tm),
        in_specs=[
            pl.BlockSpec((1, tm, width), lambda bi, t: (bi, t, q_col // width)),
            pl.BlockSpec((1, m, width), lambda bi, t: (bi, 0, 0)),
            pl.BlockSpec((1, m, width), lambda bi, t: (bi, 0, 0)),
            pl.BlockSpec((1, width), lambda bi, t: (0, 0)),
        ],
        out_specs=pl.BlockSpec((1, tm, width), lambda bi, t: (bi, t, 0)),
        out_shape=jax.ShapeDtypeStruct((b, s, width), BF16),
        scratch_shapes=[pltpu.VMEM((width // HEAD_DIM, m, 2 * HEAD_DIM), BF16)],
        compiler_params=_cparams(("parallel", "arbitrary")),
        name="memory_attention",
    )(proj, k, v, gain)


def _dft_matrices(seq):
    r1 = DFT_R1
    r2 = seq // r1
    sub = DFT_SUB
    k2 = np.arange(r2, dtype=np.int64)
    s2 = np.arange(r2, dtype=np.int64)
    s1 = np.arange(r1, dtype=np.int64)
    k1 = np.arange(r1, dtype=np.int64)
    th = 2.0 * np.pi * ((k2[:, None] * s2[None, :]) % r2).astype(np.float64) / r2
    fa = np.stack([np.cos(th), -np.sin(th)], axis=0)
    ma = np.einsum("cks,ab->ckasb", fa, np.eye(sub))
    ma = jnp.asarray(ma.reshape(2 * r2 * sub, r2 * sub), BF16)
    tw = 2.0 * np.pi * ((s1[:, None] * k2[None, :]) % seq).astype(np.float64) / seq
    tw = tw.reshape(r1 // sub, sub, r2).transpose(0, 2, 1).reshape(r1 // sub, r2 * sub, 1)
    tw_cos = jnp.broadcast_to(jnp.asarray(np.cos(tw), F32), (r1 // sub, r2 * sub, HEAD_DIM))
    tw_sin = jnp.broadcast_to(jnp.asarray(np.sin(tw), F32), (r1 // sub, r2 * sub, HEAD_DIM))
    ph = 2.0 * np.pi * ((k1[:, None] * s1[None, :]) % r1).astype(np.float64) / r1
    g = np.stack([np.cos(ph), np.sin(ph)], 0)
    mc = np.einsum("cks,ab->kacbs", g, np.eye(sub))
    mc = jnp.asarray(mc.reshape(r1 * sub, 2 * sub * r1), BF16)
    cc = np.arange(HEAD_DIM, dtype=np.int64)
    pc = 2.0 * np.pi * ((cc[:, None] * cc[None, :]) % HEAD_DIM).astype(np.float64) / HEAD_DIM
    mch = np.concatenate([np.cos(pc), -np.sin(pc)], axis=0) / np.sqrt(float(seq) * HEAD_DIM)
    return ma, tw_cos, tw_sin, mc, jnp.asarray(mch, BF16)


def _pack_bf16_pair(lo, hi):
    def bf16_bits(x):
        u = lax.bitcast_convert_type(x, jnp.uint32)
        return (u + (jnp.uint32(0x7FFF) + ((u >> 16) & jnp.uint32(1)))) >> 16
    return bf16_bits(lo) | (bf16_bits(hi) << 16)


def _unpack_bf16_pair(p):
    lo = lax.bitcast_convert_type(p << 16, F32)
    hi = lax.bitcast_convert_type(p & jnp.uint32(0xFFFF0000), F32)
    return lo, hi


def _fft_a_kernel(u_ref, ma_ref, twc_ref, tws_ref, mch_ref, wf_ref, t_ref):
    r2, sub, width = u_ref.shape[1], u_ref.shape[2], u_ref.shape[3]
    xb = u_ref[0].reshape(r2 * sub, width).astype(BF16)
    res = jnp.dot(ma_ref[...], xb, preferred_element_type=F32)
    half = r2 * sub
    tw_cos, tw_sin = twc_ref[0], tws_ref[0]
    for g in range(width // HEAD_DIM):
        sl = slice(g * HEAD_DIM, (g + 1) * HEAD_DIM)
        re, im = res[:half, sl], res[half:, sl]
        t_re = re * tw_cos + im * tw_sin
        t_im = im * tw_cos - re * tw_sin
        gg = jnp.dot(mch_ref[...], wf_ref[g], preferred_element_type=F32)
        g_re, g_im = gg[:HEAD_DIM], gg[HEAD_DIM:]
        blk = jnp.concatenate([jnp.concatenate([g_re, g_im], axis=1),
                               jnp.concatenate([-g_im, g_re], axis=1)], axis=0).astype(BF16)
        t2 = jnp.dot(jnp.concatenate([t_re, t_im], axis=1).astype(BF16), blk,
                     preferred_element_type=F32)
        packed = _pack_bf16_pair(t2[:, :HEAD_DIM], t2[:, HEAD_DIM:])
        t_ref[0, :, :, sl] = packed.reshape(r2, sub, HEAD_DIM)


def _fft_stage_a(u4, ma, tw_cos, tw_sin, mch, w_f, layer):
    b, r2, r1, width = u4.shape
    sub = DFT_SUB
    groups = width // HEAD_DIM
    tw_spec = pl.BlockSpec((1, r2 * sub, HEAD_DIM), lambda t, bi: (t, 0, 0))
    return pl.pallas_call(
        _fft_a_kernel,
        grid=(r1 // sub, b),
        in_specs=[
            pl.BlockSpec((1, r2, sub, width), lambda t, bi: (bi, 0, t, 0)),
            pl.BlockSpec(ma.shape, lambda t, bi: (0, 0)),
            tw_spec, tw_spec,
            pl.BlockSpec(mch.shape, lambda t, bi: (0, 0)),
            pl.BlockSpec((None, groups, HEAD_DIM, HEAD_DIM), lambda t, bi: (layer, 0, 0, 0)),
        ],
        out_specs=pl.BlockSpec((1, r2, sub, width), lambda t, bi: (bi, 0, t, 0)),
        out_shape=jax.ShapeDtypeStruct((b, r2, r1, width), jnp.uint32),
        compiler_params=_cparams(("arbitrary", "arbitrary")),
        name="fourier_stage_a",
    )(u4, ma, tw_cos, tw_sin, mch, w_f)


def _fft_c_kernel(t_ref, mc_ref, g_ref, o_ref):
    sub, r1, width = t_ref.shape[1], t_ref.shape[2], t_ref.shape[3]
    tp = t_ref[0].reshape(sub * r1, width)
    tb = jnp.concatenate(_unpack_bf16_pair(tp), axis=0).astype(BF16)
    res = jnp.dot(mc_ref[...], tb, preferred_element_type=F32)
    for g in range(width // HEAD_DIM):
        sl = slice(g * HEAD_DIM, (g + 1) * HEAD_DIM)
        z = _rms(res[:, sl]) * g_ref[:, sl]
        o_ref[0, :, :, sl] = z.reshape(r1, sub, HEAD_DIM)


def _fft_stage_c(t4, mc, gain):
    b, r2, r1, width = t4.shape
    sub = DFT_SUB
    return pl.pallas_call(
        _fft_c_kernel,
        grid=(r2 // sub, b),
        in_specs=[
            pl.BlockSpec((1, sub, r1, width), lambda t, bi: (bi, t, 0, 0)),
            pl.BlockSpec(mc.shape, lambda t, bi: (0, 0)),
            pl.BlockSpec((1, width), lambda t, bi: (0, 0)),
        ],
        out_specs=pl.BlockSpec((1, r1, sub, width), lambda t, bi: (bi, 0, t, 0)),
        out_shape=jax.ShapeDtypeStruct((b, r1, r2, width), F32),
        compiler_params=_cparams(("arbitrary", "arbitrary")),
        name="fourier_stage_c",
    )(t4, mc, gain)


def _outproj_kernel(x_ref, ya_ref, yf_ref, ym_ref, w_ref, o_ref):
    na = ya_ref.shape[1]
    nf = yf_ref.shape[1]
    acc = x_ref[...]
    acc = acc + jnp.dot(ya_ref[...], w_ref[:na, :], preferred_element_type=F32)
    acc = acc + jnp.dot(yf_ref[...].astype(BF16), w_ref[na:na + nf, :], preferred_element_type=F32)
    acc = acc + jnp.dot(ym_ref[...], w_ref[na + nf:, :], preferred_element_type=F32)
    o_ref[...] = acc


def _out_projection(x2, y_na, y_f, y_m, w_out, layer, tm=512):
    n, d = x2.shape
    row = lambda i: (i, 0)
    return pl.pallas_call(
        _outproj_kernel,
        grid=(n // tm,),
        in_specs=[
            pl.BlockSpec((tm, d), row),
            pl.BlockSpec((tm, y_na.shape[1]), row),
            pl.BlockSpec((tm, y_f.shape[1]), row),
            pl.BlockSpec((tm, y_m.shape[1]), row),
            pl.BlockSpec((None,) + w_out.shape[1:], lambda i: (layer, 0, 0)),
        ],
        out_specs=pl.BlockSpec((tm, d), row),
        out_shape=jax.ShapeDtypeStruct((n, d), F32),
        compiler_params=_cparams(("parallel",)),
        name="out_projection",
    )(x2, y_na, y_f, y_m, w_out)


def _ffn_kernel(x_ref, g_ref, wg_ref, wu_ref, wd_ref, o_ref, h_scr):
    f = pl.program_id(1)

    @pl.when(f == 0)
    def _():
        x = x_ref[...]
        h_scr[...] = (_rms(x) * g_ref[...]).astype(BF16)
        o_ref[...] = x

    h = h_scr[...]
    gate = jnp.dot(h, wg_ref[...], preferred_element_type=F32)
    up = jnp.dot(h, wu_ref[...], preferred_element_type=F32)
    act = (gate * jax.nn.sigmoid(gate) * up).astype(BF16)
    o_ref[...] += jnp.dot(act, wd_ref[...], preferred_element_type=F32)


def _ffn(x2, gain, w_gate, w_up, w_down, layer, tm=1024, tf=512):
    n, d = x2.shape
    d_ff = w_gate.shape[2]
    row = lambda i, f: (i, 0)
    return pl.pallas_call(
        _ffn_kernel,
        grid=(n // tm, d_ff // tf),
        in_specs=[
            pl.BlockSpec((tm, d), row),
            pl.BlockSpec((1, d), lambda i, f: (0, 0)),
            pl.BlockSpec((None, d, tf), lambda i, f: (layer, 0, f)),
            pl.BlockSpec((None, d, tf), lambda i, f: (layer, 0, f)),
            pl.BlockSpec((None, tf, d), lambda i, f: (layer, f, 0)),
        ],
        out_specs=pl.BlockSpec((tm, d), row),
        out_shape=jax.ShapeDtypeStruct((n, d), F32),
        scratch_shapes=[pltpu.VMEM((tm, d), BF16)],
        compiler_params=_cparams(("parallel", "arbitrary"), vmem=56 * MIB),
        name="swiglu_ffn",
    )(x2, gain, w_gate, w_up, w_down)


def kernel(x, mem, attn_norm, w_in, na_q_norm, na_k_norm, na_rpb, w_fourier, mem_norm, w_mem_kv,
           mem_q_norm, mem_k_norm, out_norm, w_out, ffn_norm, w_gate, w_up, w_down):
    b, s, d = x.shape
    m = mem.shape[1]
    depth = w_in.shape[0]
    n_mem_heads = w_mem_kv.shape[2] // 2 // HEAD_DIM
    fourier_width = w_fourier.shape[1] * HEAD_DIM
    mem_width = n_mem_heads * HEAD_DIM
    na_width = (w_in.shape[2] - fourier_width - mem_width) // 3
    na_heads = na_width // HEAD_DIM
    q_scale = HEAD_DIM ** -0.5 * LOG2E

    ma, tw_cos, tw_sin, mc, mch = _dft_matrices(s)
    x2 = x.reshape(b * s, d)
    mem2 = mem.reshape(b * m, d)
    w_in_b, w_kv_b, w_f_b, w_out_b = (w.astype(BF16) for w in (w_in, w_mem_kv, w_fourier, w_out))
    w_gate_b, w_up_b, w_down_b = (w.astype(BF16) for w in (w_gate, w_up, w_down))
    first_mem_head = (3 * na_width + fourier_width) // HEAD_DIM
    normed_heads = frozenset(range(2 * na_heads)) | frozenset(
        range(first_mem_head, first_mem_head + n_mem_heads))

    for l in range(depth):
        head_gain = jnp.concatenate([
            jnp.tile(na_q_norm[l] * q_scale, na_heads),
            jnp.tile(na_k_norm[l], na_heads),
            jnp.ones((na_width + fourier_width,), F32),
            jnp.tile(mem_q_norm[l] * q_scale, n_mem_heads),
        ]).reshape(1, -1)
        proj, u_f = _in_projection(x2, attn_norm[l].reshape(1, d), w_in_b, head_gain, normed_heads, l,
                                   u_col=3 * na_width, u_width=fourier_width)
        proj = proj.reshape(b, s, -1)

        gain_out = out_norm[l].reshape(1, -1)
        y_na = _neighbourhood_attention(proj, _na_bias_blocks(na_rpb[l]), gain_out[:, :na_width], na_heads)

        k_m, v_m = _memory_kv(mem2, mem_norm[l].reshape(1, d), w_kv_b,
                              jnp.tile(mem_k_norm[l], n_mem_heads).reshape(1, -1), l)
        y_m = _memory_attention(proj, 3 * na_width + fourier_width, k_m.reshape(b, m, mem_width),
                                v_m.reshape(b, m, mem_width), gain_out[:, na_width + fourier_width:])

        t5 = _fft_stage_a(u_f.reshape(b, s // DFT_R1, DFT_R1, fourier_width), ma, tw_cos, tw_sin,
                          mch, w_f_b, l)
        y_f = _fft_stage_c(t5, mc, gain_out[:, na_width:na_width + fourier_width])

        x2 = _out_projection(x2, y_na.reshape(b * s, na_width), y_f.reshape(b * s, fourier_width),
                             y_m.reshape(b * s, mem_width), w_out_b, l)
        x2 = _ffn(x2, ffn_norm[l].reshape(1, d), w_gate_b, w_up_b, w_down_b, l)
    return x2.reshape(b, s, d)
```

```python
import functools

import numpy as np
import jax
import jax.numpy as jnp
from jax import lax
from jax.experimental import pallas as pl
from jax.experimental.pallas import tpu as pltpu

F32 = jnp.float32
BF16 = jnp.bfloat16

EPS = 1e-6
HEAD_DIM = 128
MXU_N = 256
GRID_W = 64
NA_KH = 8
NA_KW = 16
NA_ROWS_PER_BLOCK = 8
NA_HALO_ROWS = 4
NA_HEADS_PER_STEP = 4
MASK_VALUE = -1e30
NA_INVALID_ROW = 2 * NA_KH - 1
LOG2E = 1.4426950408889634

DFT_R1 = 64
DFT_SUB = 8

MIB = 1024 * 1024
VMEM_LIMIT = 48 * MIB


def _cparams(sem, vmem=VMEM_LIMIT):
    return pltpu.CompilerParams(dimension_semantics=sem, vmem_limit_bytes=vmem)


def _rms(x, eps=EPS):
    return x * lax.rsqrt(jnp.mean(x * x, axis=-1, keepdims=True) + eps)


def _head_norm_store(dst_ref, acc, gain, n_heads, col0=0):
    for h in range(n_heads):
        sl = slice(col0 + h * HEAD_DIM, col0 + (h + 1) * HEAD_DIM)
        dl = slice(h * HEAD_DIM, (h + 1) * HEAD_DIM)
        dst_ref[:, dl] = (_rms(acc[:, sl]) * gain[:, sl]).astype(dst_ref.dtype)


def _inproj_kernel(x_ref, g_ref, w_ref, hg_ref, p_ref, u_ref, *, normed_heads, u_col, row_chunks):
    rows = x_ref.shape[0] // row_chunks
    u_width = u_ref.shape[1]
    heads_per_chunk = MXU_N // HEAD_DIM
    chunks = sorted(range(w_ref.shape[1] // MXU_N),
                    key=lambda c: not any(c * heads_per_chunk + hh in normed_heads
                                          for hh in range(heads_per_chunk)))
    for r in range(row_chunks):
        rs = slice(r * rows, (r + 1) * rows)
        h = (_rms(x_ref[rs, :]) * g_ref[...]).astype(BF16)
        for c in chunks:
            acc = jnp.dot(h, w_ref[:, c * MXU_N:(c + 1) * MXU_N], preferred_element_type=F32)
            for hh in range(MXU_N // HEAD_DIM):
                blk = acc[:, hh * HEAD_DIM:(hh + 1) * HEAD_DIM]
                head = c * (MXU_N // HEAD_DIM) + hh
                sl = slice(head * HEAD_DIM, (head + 1) * HEAD_DIM)
                if head in normed_heads:
                    p_ref[rs, sl] = (_rms(blk) * hg_ref[:, sl]).astype(p_ref.dtype)
                else:
                    p_ref[rs, sl] = blk.astype(p_ref.dtype)
                if u_col <= sl.start < u_col + u_width:
                    u_ref[rs, sl.start - u_col:sl.stop - u_col] = blk


def _in_projection(x2, gain, w_in, head_gain, normed_heads, layer, u_col, u_width, tm=512, row_chunks=2):
    n, d = x2.shape
    d_in = w_in.shape[2]
    return pl.pallas_call(
        functools.partial(_inproj_kernel, normed_heads=normed_heads, u_col=u_col, row_chunks=row_chunks),
        grid=(n // tm,),
        in_specs=[
            pl.BlockSpec((tm, d), lambda i: (i, 0)),
            pl.BlockSpec((1, d), lambda i: (0, 0)),
            pl.BlockSpec((None, d, d_in), lambda i: (layer, 0, 0)),
            pl.BlockSpec((1, d_in), lambda i: (0, 0)),
        ],
        out_specs=[
            pl.BlockSpec((tm, d_in), lambda i: (i, 0)),
            pl.BlockSpec((tm, u_width), lambda i: (i, 0)),
        ],
        out_shape=[
            jax.ShapeDtypeStruct((n, d_in), BF16),
            jax.ShapeDtypeStruct((n, u_width), F32),
        ],
        compiler_params=_cparams(("parallel",)),
        name="in_projection",
    )(x2, gain, w_in, head_gain)


def _na_build_bias(variant, e_ref, bias_scr, hh):
    rq, halo = NA_ROWS_PER_BLOCK, NA_HALO_ROWS
    half_rows = rq // 2
    n_pairs = (half_rows + NA_KH) // 2
    lane = lax.broadcasted_iota(jnp.int32, (GRID_W, 2 * GRID_W), 1)
    for il in range(rq):
        jl0 = jnp.where(variant == 0, max(il, halo), jnp.where(variant == 2, min(il, halo), il))
        half = il // half_rows
        rows = slice((il % half_rows) * GRID_W, (il % half_rows + 1) * GRID_W)
        for pair in range(n_pairs):
            blocks = []
            for jl in (half * halo + 2 * pair, half * halo + 2 * pair + 1):
                inside = jnp.logical_and(jl >= jl0, jl < jl0 + NA_KH)
                idx = jnp.where(inside, jl - il - halo + NA_KH - 1, NA_INVALID_ROW)
                blocks.append(e_ref[hh, idx])
            bias_scr[hh, half, rows, pair * 2 * GRID_W:(pair + 1) * 2 * GRID_W] = (
                jnp.where(lane < GRID_W, blocks[0], blocks[1]))


def _na_kernel(q_ref, kp_ref, kc_ref, kn_ref, vp_ref, vc_ref, vn_ref, e_ref, g_ref, o_ref,
               bias_scr, vx_scr):
    n_batch = q_ref.shape[0]
    n_heads = q_ref.shape[2] // HEAD_DIM
    rb = pl.program_id(1)
    last = pl.num_programs(1) - 1
    half_tokens = q_ref.shape[1] // 2
    halo_tokens = kp_ref.shape[1]
    win_tokens = half_tokens + 2 * halo_tokens
    contract_last = (((1,), (1,)), ((), ()))

    @pl.when(jnp.logical_or(rb <= 1, rb == last))
    def _():
        variant = jnp.where(rb == 0, 0, jnp.where(rb == last, 2, 1))
        for hh in range(n_heads):
            _na_build_bias(variant, e_ref, bias_scr, hh)
        vx_scr[:, :, :, HEAD_DIM:] = jnp.ones(vx_scr.shape[:3] + (HEAD_DIM,), BF16)

    k_segments = ((kp_ref, kc_ref), (kc_ref, kn_ref))
    for hh in range(n_heads):
        hs = slice(hh * HEAD_DIM, (hh + 1) * HEAD_DIM)
        for b in range(n_batch):
            row0 = 0
            for v_ref in (vp_ref, vc_ref, vn_ref):
                vx_scr[b, hh, row0:row0 + v_ref.shape[1], :HEAD_DIM] = v_ref[b, :, hs]
                row0 += v_ref.shape[1]
            for half in range(2):
                rows = slice(half * half_tokens, (half + 1) * half_tokens)
                keys = slice(half * halo_tokens, half * halo_tokens + win_tokens)
                q = q_ref[b, rows, hs]
                s = jnp.concatenate(
                    [lax.dot_general(q, k_ref[b, :, hs], contract_last, preferred_element_type=F32)
                     for k_ref in k_segments[half]], axis=-1)
                s = s + bias_scr[hh, half]
                m = jnp.max(s, axis=-1, keepdims=True)
                pb = jnp.exp2((s - m).astype(BF16))
                ol = jnp.dot(pb, vx_scr[b, hh, keys, :], preferred_element_type=F32)
                o = ol[:, :HEAD_DIM] * (1.0 / ol[:, HEAD_DIM:])
                o_ref[b, rows, hs] = (_rms(o) * g_ref[:, hs]).astype(o_ref.dtype)


def _na_bias_blocks(rpb):
    n_heads = rpb.shape[0]
    kw = min(NA_KW, GRID_W)
    col = np.arange(GRID_W)
    col_start = np.clip(col - kw // 2, 0, GRID_W - kw)
    col_ok = (col[None, :] >= col_start[:, None]) & (col[None, :] < col_start[:, None] + kw)
    col_idx = np.clip(col[None, :] - col[:, None] + NA_KW - 1, 0, 2 * NA_KW - 2)
    onehot = jnp.asarray(col_idx[None] == np.arange(2 * NA_KW - 1)[:, None, None], F32)
    picked = jnp.einsum("hde,eqk->hdqk", rpb, onehot, precision=lax.Precision.HIGHEST)
    toe = jnp.where(col_ok[None, None], picked * LOG2E, MASK_VALUE)
    toe = jnp.concatenate([toe, jnp.full((n_heads, 1, GRID_W, GRID_W), MASK_VALUE, F32)], axis=1)
    return jnp.concatenate([toe, toe], axis=-1)


def _neighbourhood_attention(proj, bias_blocks, gain, n_heads):
    b, s, _ = proj.shape
    width = n_heads * HEAD_DIM
    tq = NA_ROWS_PER_BLOCK * GRID_W
    th = NA_HALO_ROWS * GRID_W
    n_blocks = s // tq
    n_halo_blocks = s // th
    per = tq // th
    assert n_blocks >= 3 and NA_KH == 2 * NA_HALO_ROWS == NA_ROWS_PER_BLOCK

    hps = NA_HEADS_PER_STEP
    wb = hps * HEAD_DIM
    assert n_heads % hps == 0

    def cur(col0):
        return pl.BlockSpec((b, tq, wb), lambda h, rb: (0, rb, col0 + h))

    def prev(col0):
        return pl.BlockSpec((b, th, wb), lambda h, rb: (0, jnp.maximum(per * rb - 1, 0), col0 + h))

    def nxt(col0):
        return pl.BlockSpec((b, th, wb),
                            lambda h, rb: (0, jnp.minimum(per * rb + per, n_halo_blocks - 1), col0 + h))

    k0, v0 = n_heads // hps, 2 * n_heads // hps
    return pl.pallas_call(
        _na_kernel,
        grid=(n_heads // hps, n_blocks),
        in_specs=[
            cur(0), prev(k0), cur(k0), nxt(k0), prev(v0), cur(v0), nxt(v0),
            pl.BlockSpec((hps,) + bias_blocks.shape[1:], lambda h, rb: (h, 0, 0, 0)),
            pl.BlockSpec((1, wb), lambda h, rb: (0, h)),
        ],
        out_specs=cur(0),
        out_shape=jax.ShapeDtypeStruct((b, s, width), BF16),
        scratch_shapes=[pltpu.VMEM((hps, 2, tq // 2, tq // 2 + 2 * th), F32),
                        pltpu.VMEM((b, hps, tq + 2 * th, 2 * HEAD_DIM), BF16)],
        compiler_params=_cparams(("arbitrary", "arbitrary"), vmem=52 * MIB),
        name="neighbourhood_attention",
    )(proj, proj, proj, proj, proj, proj, proj, bias_blocks, gain)


def _mem_kv_kernel(m_ref, g_ref, w_ref, kg_ref, k_ref, v_ref, h_scr):
    j = pl.program_id(0)

    @pl.when(j == 0)
    def _():
        h_scr[...] = (_rms(m_ref[...]) * g_ref[...]).astype(BF16)

    acc = jnp.dot(h_scr[...], w_ref[...], preferred_element_type=F32)

    @pl.when(j == 0)
    def _():
        _head_norm_store(k_ref, acc, kg_ref[...], acc.shape[1] // HEAD_DIM)

    @pl.when(j == 1)
    def _():
        v_ref[...] = acc.astype(v_ref.dtype)


def _memory_kv(mem2, gain, w_kv, k_gain, layer):
    n, d = mem2.shape
    width = w_kv.shape[2] // 2
    full = lambda j: (0, 0)
    return pl.pallas_call(
        _mem_kv_kernel,
        grid=(2,),
        in_specs=[
            pl.BlockSpec((n, d), full),
            pl.BlockSpec((1, d), full),
            pl.BlockSpec((None, d, width), lambda j: (layer, 0, j)),
            pl.BlockSpec((1, width), full),
        ],
        out_specs=[pl.BlockSpec((n, width), full), pl.BlockSpec((n, width), full)],
        out_shape=[jax.ShapeDtypeStruct((n, width), BF16), jax.ShapeDtypeStruct((n, width), BF16)],
        scratch_shapes=[pltpu.VMEM((n, d), BF16)],
        compiler_params=_cparams(("arbitrary",)),
        name="memory_kv",
    )(mem2, gain, w_kv, k_gain)


def _mem_attn_kernel(q_ref, k_ref, v_ref, g_ref, o_ref, vx_scr):
    n_heads = q_ref.shape[2] // HEAD_DIM
    n_mem = v_ref.shape[1]
    for h in range(n_heads):
        vx_scr[h, :, :HEAD_DIM] = v_ref[0, :, h * HEAD_DIM:(h + 1) * HEAD_DIM]
        vx_scr[h, :, HEAD_DIM:] = jnp.ones((n_mem, HEAD_DIM), BF16)
    for h in range(n_heads):
        sl = slice(h * HEAD_DIM, (h + 1) * HEAD_DIM)
        s = lax.dot_general(q_ref[0, :, sl], k_ref[0, :, sl], (((1,), (1,)), ((), ())),
                            preferred_element_type=F32)
        m = jnp.max(s, axis=-1, keepdims=True)
        pb = jnp.exp2((s - m).astype(BF16))
        ol = jnp.dot(pb, vx_scr[h], preferred_element_type=F32)
        o = ol[:, :HEAD_DIM] * (1.0 / ol[:, HEAD_DIM:])
        o_ref[0, :, sl] = (_rms(o) * g_ref[:, sl]).astype(o_ref.dtype)


def _memory_attention(proj, q_col, k, v, gain, tm=1024):
    b, s, _ = proj.shape
    m, width = k.shape[1], k.shape[2]
    assert q_col % width == 0
    return pl.pallas_call(
        _mem_attn_kernel,
        grid=(b, s // tm),
        in_specs=[
            pl.BlockSpec((1, tm, width), lambda bi, t: (bi, t, q_col // width)),
            pl.BlockSpec((1, m, width), lambda bi, t: (bi, 0, 0)),
            pl.BlockSpec((1, m, width), lambda bi, t: (bi, 0, 0)),
            pl.BlockSpec((1, width), lambda bi, t: (0, 0)),
        ],
        out_specs=pl.BlockSpec((1, tm, width), lambda bi, t: (bi, t, 0)),
        out_shape=jax.ShapeDtypeStruct((b, s, width), BF16),
        scratch_shapes=[pltpu.VMEM((width // HEAD_DIM, m, 2 * HEAD_DIM), BF16)],
        compiler_params=_cparams(("parallel", "arbitrary")),
        name="memory_attention",
    )(proj, k, v, gain)


def _dft_matrices(seq):
    r1 = DFT_R1
    r2 = seq // r1
    sub = DFT_SUB
    k2 = np.arange(r2, dtype=np.int64)
    s2 = np.arange(r2, dtype=np.int64)
    s1 = np.arange(r1, dtype=np.int64)
    k1 = np.arange(r1, dtype=np.int64)
    th = 2.0 * np.pi * ((k2[:, None] * s2[None, :]) % r2).astype(np.float64) / r2
    fa = np.stack([np.cos(th), -np.sin(th)], axis=0)
    ma = np.einsum("cks,ab->ckasb", fa, np.eye(sub))
    ma = jnp.asarray(ma.reshape(2 * r2 * sub, r2 * sub), BF16)
    tw = 2.0 * np.pi * ((s1[:, None] * k2[None, :]) % seq).astype(np.float64) / seq
    tw = tw.reshape(r1 // sub, sub, r2).transpose(0, 2, 1).reshape(r1 // sub, r2 * sub, 1)
    tw_cos = jnp.broadcast_to(jnp.asarray(np.cos(tw), F32), (r1 // sub, r2 * sub, HEAD_DIM))
    tw_sin = jnp.broadcast_to(jnp.asarray(np.sin(tw), F32), (r1 // sub, r2 * sub, HEAD_DIM))
    ph = 2.0 * np.pi * ((k1[:, None] * s1[None, :]) % r1).astype(np.float64) / r1
    g = np.stack([np.cos(ph), np.sin(ph)], 0)
    mc = np.einsum("cks,ab->kacbs", g, np.eye(sub))
    mc = jnp.asarray(mc.reshape(r1 * sub, 2 * sub * r1), BF16)
    cc = np.arange(HEAD_DIM, dtype=np.int64)
    pc = 2.0 * np.pi * ((cc[:, None] * cc[None, :]) % HEAD_DIM).astype(np.float64) / HEAD_DIM
    mch = np.concatenate([np.cos(pc), -np.sin(pc)], axis=0) / np.sqrt(float(seq) * HEAD_DIM)
    return ma, tw_cos, tw_sin, mc, jnp.asarray(mch, BF16)


def _pack_bf16_pair(lo, hi):
    def bf16_bits(x):
        u = lax.bitcast_convert_type(x, jnp.uint32)
        return (u + (jnp.uint32(0x7FFF) + ((u >> 16) & jnp.uint32(1)))) >> 16
    return bf16_bits(lo) | (bf16_bits(hi) << 16)


def _unpack_bf16_pair(p):
    lo = lax.bitcast_convert_type(p << 16, F32)
    hi = lax.bitcast_convert_type(p & jnp.uint32(0xFFFF0000), F32)
    return lo, hi


def _fft_a_kernel(u_ref, ma_ref, twc_ref, tws_ref, mch_ref, wf_ref, t_ref):
    r2, sub, width = u_ref.shape[1], u_ref.shape[2], u_ref.shape[3]
    xb = u_ref[0].reshape(r2 * sub, width).astype(BF16)
    res = jnp.dot(ma_ref[...], xb, preferred_element_type=F32)
    half = r2 * sub
    tw_cos, tw_sin = twc_ref[0], tws_ref[0]
    for g in range(width // HEAD_DIM):
        sl = slice(g * HEAD_DIM, (g + 1) * HEAD_DIM)
        re, im = res[:half, sl], res[half:, sl]
        t_re = re * tw_cos + im * tw_sin
        t_im = im * tw_cos - re * tw_sin
        gg = jnp.dot(mch_ref[...], wf_ref[g], preferred_element_type=F32)
        g_re, g_im = gg[:HEAD_DIM], gg[HEAD_DIM:]
        blk = jnp.concatenate([jnp.concatenate([g_re, g_im], axis=1),
                               jnp.concatenate([-g_im, g_re], axis=1)], axis=0).astype(BF16)
        t2 = jnp.dot(jnp.concatenate([t_re, t_im], axis=1).astype(BF16), blk,
                     preferred_element_type=F32)
        packed = _pack_bf16_pair(t2[:, :HEAD_DIM], t2[:, HEAD_DIM:])
        t_ref[0, :, :, sl] = packed.reshape(r2, sub, HEAD_DIM)


def _fft_stage_a(u4, ma, tw_cos, tw_sin, mch, w_f, layer):
    b, r2, r1, width = u4.shape
    sub = DFT_SUB
    groups = width // HEAD_DIM
    tw_spec = pl.BlockSpec((1, r2 * sub, HEAD_DIM), lambda t, bi: (t, 0, 0))
    return pl.pallas_call(
        _fft_a_kernel,
        grid=(r1 // sub, b),
        in_specs=[
            pl.BlockSpec((1, r2, sub, width), lambda t, bi: (bi, 0, t, 0)),
            pl.BlockSpec(ma.shape, lambda t, bi: (0, 0)),
            tw_spec, tw_spec,
            pl.BlockSpec(mch.shape, lambda t, bi: (0, 0)),
            pl.BlockSpec((None, groups, HEAD_DIM, HEAD_DIM), lambda t, bi: (layer, 0, 0, 0)),
        ],
        out_specs=pl.BlockSpec((1, r2, sub, width), lambda t, bi: (bi, 0, t, 0)),
        out_shape=jax.ShapeDtypeStruct((b, r2, r1, width), jnp.uint32),
        compiler_params=_cparams(("arbitrary", "arbitrary")),
        name="fourier_stage_a",
    )(u4, ma, tw_cos, tw_sin, mch, w_f)


def _fft_c_kernel(t_ref, mc_ref, g_ref, o_ref):
    sub = DFT_SUB
    r1, width = t_ref.shape[2], t_ref.shape[3]
    for j in range(t_ref.shape[1] // sub):
        ks = slice(j * sub, (j + 1) * sub)
        tp = t_ref[0, ks].reshape(sub * r1, width)
        tb = jnp.concatenate(_unpack_bf16_pair(tp), axis=0).astype(BF16)
        res = jnp.dot(mc_ref[...], tb, preferred_element_type=F32)
        for g in range(width // HEAD_DIM):
            sl = slice(g * HEAD_DIM, (g + 1) * HEAD_DIM)
            z = _rms(res[:, sl]) * g_ref[:, sl]
            o_ref[0, :, ks, sl] = z.reshape(r1, sub, HEAD_DIM)


def _fft_stage_c(t4, mc, gain, k2_tiles=2):
    b, r2, r1, width = t4.shape
    sub = DFT_SUB * k2_tiles
    return pl.pallas_call(
        _fft_c_kernel,
        grid=(r2 // sub, b),
        in_specs=[
            pl.BlockSpec((1, sub, r1, width), lambda t, bi: (bi, t, 0, 0)),
            pl.BlockSpec(mc.shape, lambda t, bi: (0, 0)),
            pl.BlockSpec((1, width), lambda t, bi: (0, 0)),
        ],
        out_specs=pl.BlockSpec((1, r1, sub, width), lambda t, bi: (bi, 0, t, 0)),
        out_shape=jax.ShapeDtypeStruct((b, r1, r2, width), F32),
        compiler_params=_cparams(("arbitrary", "arbitrary")),
        name="fourier_stage_c",
    )(t4, mc, gain)


def _outproj_kernel(x_ref, ya_ref, yf_ref, ym_ref, w_ref, o_ref):
    na = ya_ref.shape[1]
    nf = yf_ref.shape[1]
    acc = x_ref[...]
    acc = acc + jnp.dot(ya_ref[...], w_ref[:na, :], preferred_element_type=F32)
    acc = acc + jnp.dot(yf_ref[...].astype(BF16), w_ref[na:na + nf, :], preferred_element_type=F32)
    acc = acc + jnp.dot(ym_ref[...], w_ref[na + nf:, :], preferred_element_type=F32)
    o_ref[...] = acc


def _out_projection(x2, y_na, y_f, y_m, w_out, layer, tm=512):
    n, d = x2.shape
    row = lambda i: (i, 0)
    return pl.pallas_call(
        _outproj_kernel,
        grid=(n // tm,),
        in_specs=[
            pl.BlockSpec((tm, d), row),
            pl.BlockSpec((tm, y_na.shape[1]), row),
            pl.BlockSpec((tm, y_f.shape[1]), row),
            pl.BlockSpec((tm, y_m.shape[1]), row),
            pl.BlockSpec((None,) + w_out.shape[1:], lambda i: (layer, 0, 0)),
        ],
        out_specs=pl.BlockSpec((tm, d), row),
        out_shape=jax.ShapeDtypeStruct((n, d), F32),
        compiler_params=_cparams(("parallel",)),
        name="out_projection",
    )(x2, y_na, y_f, y_m, w_out)


def _ffn_kernel(x_ref, g_ref, wg_ref, wu_ref, wd_ref, o_ref, h_scr):
    f = pl.program_id(1)

    @pl.when(f == 0)
    def _():
        x = x_ref[...]
        h_scr[...] = (_rms(x) * g_ref[...]).astype(BF16)
        o_ref[...] = x

    h = h_scr[...]
    gate = jnp.dot(h, wg_ref[...], preferred_element_type=F32)
    up = jnp.dot(h, wu_ref[...], preferred_element_type=F32)
    act = (gate * jax.nn.sigmoid(gate) * up).astype(BF16)
    o_ref[...] += jnp.dot(act, wd_ref[...], preferred_element_type=F32)


def _ffn(x2, gain, w_gate, w_up, w_down, layer, tm=1024, tf=512):
    n, d = x2.shape
    d_ff = w_gate.shape[2]
    row = lambda i, f: (i, 0)
    return pl.pallas_call(
        _ffn_kernel,
        grid=(n // tm, d_ff // tf),
        in_specs=[
            pl.BlockSpec((tm, d), row),
            pl.BlockSpec((1, d), lambda i, f: (0, 0)),
            pl.BlockSpec((None, d, tf), lambda i, f: (layer, 0, f)),
            pl.BlockSpec((None, d, tf), lambda i, f: (layer, 0, f)),
            pl.BlockSpec((None, tf, d), lambda i, f: (layer, f, 0)),
        ],
        out_specs=pl.BlockSpec((tm, d), row),
        out_shape=jax.ShapeDtypeStruct((n, d), F32),
        scratch_shapes=[pltpu.VMEM((tm, d), BF16)],
        compiler_params=_cparams(("parallel", "arbitrary"), vmem=56 * MIB),
        name="swiglu_ffn",
    )(x2, gain, w_gate, w_up, w_down)


def kernel(x, mem, attn_norm, w_in, na_q_norm, na_k_norm, na_rpb, w_fourier, mem_norm, w_mem_kv,
           mem_q_norm, mem_k_norm, out_norm, w_out, ffn_norm, w_gate, w_up, w_down):
    b, s, d = x.shape
    m = mem.shape[1]
    depth = w_in.shape[0]
    n_mem_heads = w_mem_kv.shape[2] // 2 // HEAD_DIM
    fourier_width = w_fourier.shape[1] * HEAD_DIM
    mem_width = n_mem_heads * HEAD_DIM
    na_width = (w_in.shape[2] - fourier_width - mem_width) // 3
    na_heads = na_width // HEAD_DIM
    q_scale = HEAD_DIM ** -0.5 * LOG2E

    ma, tw_cos, tw_sin, mc, mch = _dft_matrices(s)
    x2 = x.reshape(b * s, d)
    mem2 = mem.reshape(b * m, d)
    w_in_b, w_kv_b, w_f_b, w_out_b = (w.astype(BF16) for w in (w_in, w_mem_kv, w_fourier, w_out))
    w_gate_b, w_up_b, w_down_b = (w.astype(BF16) for w in (w_gate, w_up, w_down))
    first_mem_head = (3 * na_width + fourier_width) // HEAD_DIM
    normed_heads = frozenset(range(2 * na_heads)) | frozenset(
        range(first_mem_head, first_mem_head + n_mem_heads))

    for l in range(depth):
        head_gain = jnp.concatenate([
            jnp.tile(na_q_norm[l] * q_scale, na_heads),
            jnp.tile(na_k_norm[l], na_heads),
            jnp.ones((na_width + fourier_width,), F32),
            jnp.tile(mem_q_norm[l] * q_scale, n_mem_heads),
        ]).reshape(1, -1)
        proj, u_f = _in_projection(x2, attn_norm[l].reshape(1, d), w_in_b, head_gain, normed_heads, l,
                                   u_col=3 * na_width, u_width=fourier_width)
        proj = proj.reshape(b, s, -1)

        gain_out = out_norm[l].reshape(1, -1)
        y_na = _neighbourhood_attention(proj, _na_bias_blocks(na_rpb[l]), gain_out[:, :na_width], na_heads)

        k_m, v_m = _memory_kv(mem2, mem_norm[l].reshape(1, d), w_kv_b,
                              jnp.tile(mem_k_norm[l], n_mem_heads).reshape(1, -1), l)
        y_m = _memory_attention(proj, 3 * na_width + fourier_width, k_m.reshape(b, m, mem_width),
                                v_m.reshape(b, m, mem_width), gain_out[:, na_width + fourier_width:])

        t5 = _fft_stage_a(u_f.reshape(b, s // DFT_R1, DFT_R1, fourier_width), ma, tw_cos, tw_sin,
                          mch, w_f_b, l)
        y_f = _fft_stage_c(t5, mc, gain_out[:, na_width:na_width + fourier_width])

        x2 = _out_projection(x2, y_na.reshape(b * s, na_width), y_f.reshape(b * s, fourier_width),
                             y_m.reshape(b * s, mem_width), w_out_b, l)
        x2 = _ffn(x2, ffn_norm[l].reshape(1, d), w_gate_b, w_up_b, w_down_b, l)
    return x2.reshape(b, s, d)
```

```python
import functools

import numpy as np
import jax
import jax.numpy as jnp
from jax import lax
from jax.experimental import pallas as pl
from jax.experimental.pallas import tpu as pltpu

F32 = jnp.float32
BF16 = jnp.bfloat16

EPS = 1e-6
HEAD_DIM = 128
MXU_N = 256
GRID_W = 64
NA_KH = 8
NA_KW = 16
NA_ROWS_PER_BLOCK = 8
NA_HALO_ROWS = 4
NA_HEADS_PER_STEP = 4
MASK_VALUE = -1e30
NA_INVALID_ROW = 2 * NA_KH - 1
LOG2E = 1.4426950408889634

DFT_R1 = 64
DFT_SUB = 8

MIB = 1024 * 1024
VMEM_LIMIT = 48 * MIB


def _cparams(sem, vmem=VMEM_LIMIT):
    return pltpu.CompilerParams(dimension_semantics=sem, vmem_limit_bytes=vmem)


def _rms(x, eps=EPS):
    return x * lax.rsqrt(jnp.mean(x * x, axis=-1, keepdims=True) + eps)


def _head_norm_store(dst_ref, acc, gain, n_heads, col0=0):
    for h in range(n_heads):
        sl = slice(col0 + h * HEAD_DIM, col0 + (h + 1) * HEAD_DIM)
        dl = slice(h * HEAD_DIM, (h + 1) * HEAD_DIM)
        dst_ref[:, dl] = (_rms(acc[:, sl]) * gain[:, sl]).astype(dst_ref.dtype)


def _inproj_kernel(x_ref, g_ref, w_ref, hg_ref, p_ref, u_ref, *, normed_heads, u_col, row_chunks):
    rows = x_ref.shape[0] // row_chunks
    u_width = u_ref.shape[1]
    heads_per_chunk = MXU_N // HEAD_DIM
    chunks = sorted(range(w_ref.shape[1] // MXU_N),
                    key=lambda c: not any(c * heads_per_chunk + hh in normed_heads
                                          for hh in range(heads_per_chunk)))
    for r in range(row_chunks):
        rs = slice(r * rows, (r + 1) * rows)
        h = (_rms(x_ref[rs, :]) * g_ref[...]).astype(BF16)
        for c in chunks:
            acc = jnp.dot(h, w_ref[:, c * MXU_N:(c + 1) * MXU_N], preferred_element_type=F32)
            for hh in range(heads_per_chunk):
                blk = acc[:, hh * HEAD_DIM:(hh + 1) * HEAD_DIM]
                head = c * heads_per_chunk + hh
                sl = slice(head * HEAD_DIM, (head + 1) * HEAD_DIM)
                if head in normed_heads:
                    p_ref[rs, sl] = (_rms(blk) * hg_ref[:, sl]).astype(p_ref.dtype)
                else:
                    p_ref[rs, sl] = blk.astype(p_ref.dtype)
                if u_col <= sl.start < u_col + u_width:
                    u_ref[rs, sl.start - u_col:sl.stop - u_col] = blk


def _in_projection(x2, gain, w_in, head_gain, normed_heads, layer, u_col, u_width, tm=512, row_chunks=2):
    n, d = x2.shape
    d_in = w_in.shape[2]
    return pl.pallas_call(
        functools.partial(_inproj_kernel, normed_heads=normed_heads, u_col=u_col, row_chunks=row_chunks),
        grid=(n // tm,),
        in_specs=[
            pl.BlockSpec((tm, d), lambda i: (i, 0)),
            pl.BlockSpec((1, d), lambda i: (0, 0)),
            pl.BlockSpec((None, d, d_in), lambda i: (layer, 0, 0)),
            pl.BlockSpec((1, d_in), lambda i: (0, 0)),
        ],
        out_specs=[
            pl.BlockSpec((tm, d_in), lambda i: (i, 0)),
            pl.BlockSpec((tm, u_width), lambda i: (i, 0)),
        ],
        out_shape=[
            jax.ShapeDtypeStruct((n, d_in), BF16),
            jax.ShapeDtypeStruct((n, u_width), F32),
        ],
        compiler_params=_cparams(("parallel",)),
        name="in_projection",
    )(x2, gain, w_in, head_gain)


def _na_build_bias(variant, e_ref, bias_scr, hh):
    rq, halo = NA_ROWS_PER_BLOCK, NA_HALO_ROWS
    half_rows = rq // 2
    n_pairs = (half_rows + NA_KH) // 2
    lane = lax.broadcasted_iota(jnp.int32, (GRID_W, 2 * GRID_W), 1)
    for il in range(rq):
        jl0 = jnp.where(variant == 0, max(il, halo), jnp.where(variant == 2, min(il, halo), il))
        half = il // half_rows
        rows = slice((il % half_rows) * GRID_W, (il % half_rows + 1) * GRID_W)
        for pair in range(n_pairs):
            blocks = []
            for jl in (half * halo + 2 * pair, half * halo + 2 * pair + 1):
                inside = jnp.logical_and(jl >= jl0, jl < jl0 + NA_KH)
                idx = jnp.where(inside, jl - il - halo + NA_KH - 1, NA_INVALID_ROW)
                blocks.append(e_ref[hh, idx])
            bias_scr[hh, half, rows, pair * 2 * GRID_W:(pair + 1) * 2 * GRID_W] = (
                jnp.where(lane < GRID_W, blocks[0], blocks[1]))


def _na_kernel(q_ref, kp_ref, kc_ref, kn_ref, vp_ref, vc_ref, vn_ref, e_ref, g_ref, o_ref,
               bias_scr, vx_scr):
    n_batch = q_ref.shape[0]
    n_heads = q_ref.shape[2] // HEAD_DIM
    rb = pl.program_id(1)
    last = pl.num_programs(1) - 1
    half_tokens = q_ref.shape[1] // 2
    halo_tokens = kp_ref.shape[1]
    win_tokens = half_tokens + 2 * halo_tokens
    contract_last = (((1,), (1,)), ((), ()))

    @pl.when(jnp.logical_or(rb <= 1, rb == last))
    def _():
        variant = jnp.where(rb == 0, 0, jnp.where(rb == last, 2, 1))
        for hh in range(n_heads):
            _na_build_bias(variant, e_ref, bias_scr, hh)
        vx_scr[:, :, :, HEAD_DIM:] = jnp.ones(vx_scr.shape[:3] + (HEAD_DIM,), BF16)

    k_segments = ((kp_ref, kc_ref), (kc_ref, kn_ref))
    for hh in range(n_heads):
        hs = slice(hh * HEAD_DIM, (hh + 1) * HEAD_DIM)
        for b in range(n_batch):
            row0 = 0
            for v_ref in (vp_ref, vc_ref, vn_ref):
                vx_scr[b, hh, row0:row0 + v_ref.shape[1], :HEAD_DIM] = v_ref[b, :, hs]
                row0 += v_ref.shape[1]
            for half in range(2):
                rows = slice(half * half_tokens, (half + 1) * half_tokens)
                keys = slice(half * halo_tokens, half * halo_tokens + win_tokens)
                q = q_ref[b, rows, hs]
                s = jnp.concatenate(
                    [lax.dot_general(q, k_ref[b, :, hs], contract_last, preferred_element_type=F32)
                     for k_ref in k_segments[half]], axis=-1)
                s = s + bias_scr[hh, half]
                m = jnp.max(s, axis=-1, keepdims=True)
                pb = jnp.exp2((s - m).astype(BF16))
                ol = jnp.dot(pb, vx_scr[b, hh, keys, :], preferred_element_type=F32)
                o = ol[:, :HEAD_DIM] * (1.0 / ol[:, HEAD_DIM:])
                o_ref[b, rows, hs] = (_rms(o) * g_ref[:, hs]).astype(o_ref.dtype)


def _na_bias_blocks(rpb):
    n_heads = rpb.shape[0]
    kw = min(NA_KW, GRID_W)
    col = np.arange(GRID_W)
    col_start = np.clip(col - kw // 2, 0, GRID_W - kw)
    col_ok = (col[None, :] >= col_start[:, None]) & (col[None, :] < col_start[:, None] + kw)
    col_idx = np.clip(col[None, :] - col[:, None] + NA_KW - 1, 0, 2 * NA_KW - 2)
    onehot = jnp.asarray(col_idx[None] == np.arange(2 * NA_KW - 1)[:, None, None], F32)
    picked = jnp.einsum("hde,eqk->hdqk", rpb, onehot, precision=lax.Precision.HIGHEST)
    toe = jnp.where(col_ok[None, None], picked * LOG2E, MASK_VALUE)
    toe = jnp.concatenate([toe, jnp.full((n_heads, 1, GRID_W, GRID_W), MASK_VALUE, F32)], axis=1)
    return jnp.concatenate([toe, toe], axis=-1)


def _neighbourhood_attention(proj, bias_blocks, gain, n_heads):
    b, s, _ = proj.shape
    width = n_heads * HEAD_DIM
    tq = NA_ROWS_PER_BLOCK * GRID_W
    th = NA_HALO_ROWS * GRID_W
    n_blocks = s // tq
    n_halo_blocks = s // th
    per = tq // th
    assert n_blocks >= 3 and NA_KH == 2 * NA_HALO_ROWS == NA_ROWS_PER_BLOCK

    hps = NA_HEADS_PER_STEP
    wb = hps * HEAD_DIM
    assert n_heads % hps == 0

    def cur(col0):
        return pl.BlockSpec((b, tq, wb), lambda h, rb: (0, rb, col0 + h))

    def prev(col0):
        return pl.BlockSpec((b, th, wb), lambda h, rb: (0, jnp.maximum(per * rb - 1, 0), col0 + h))

    def nxt(col0):
        return pl.BlockSpec((b, th, wb),
                            lambda h, rb: (0, jnp.minimum(per * rb + per, n_halo_blocks - 1), col0 + h))

    k0, v0 = n_heads // hps, 2 * n_heads // hps
    return pl.pallas_call(
        _na_kernel,
        grid=(n_heads // hps, n_blocks),
        in_specs=[
            cur(0), prev(k0), cur(k0), nxt(k0), prev(v0), cur(v0), nxt(v0),
            pl.BlockSpec((hps,) + bias_blocks.shape[1:], lambda h, rb: (h, 0, 0, 0)),
            pl.BlockSpec((1, wb), lambda h, rb: (0, h)),
        ],
        out_specs=cur(0),
        out_shape=jax.ShapeDtypeStruct((b, s, width), BF16),
        scratch_shapes=[pltpu.VMEM((hps, 2, tq // 2, tq // 2 + 2 * th), F32),
                        pltpu.VMEM((b, hps, tq + 2 * th, 2 * HEAD_DIM), BF16)],
        compiler_params=_cparams(("arbitrary", "arbitrary"), vmem=52 * MIB),
        name="neighbourhood_attention",
    )(proj, proj, proj, proj, proj, proj, proj, bias_blocks, gain)


def _mem_kv_kernel(m_ref, g_ref, w_ref, kg_ref, k_ref, v_ref, h_scr):
    j = pl.program_id(0)

    @pl.when(j == 0)
    def _():
        h_scr[...] = (_rms(m_ref[...]) * g_ref[...]).astype(BF16)

    acc = jnp.dot(h_scr[...], w_ref[...], preferred_element_type=F32)

    @pl.when(j == 0)
    def _():
        _head_norm_store(k_ref, acc, kg_ref[...], acc.shape[1] // HEAD_DIM)

    @pl.when(j == 1)
    def _():
        v_ref[...] = acc.astype(v_ref.dtype)


def _memory_kv(mem2, gain, w_kv, k_gain, layer):
    n, d = mem2.shape
    width = w_kv.shape[2] // 2
    full = lambda j: (0, 0)
    return pl.pallas_call(
        _mem_kv_kernel,
        grid=(2,),
        in_specs=[
            pl.BlockSpec((n, d), full),
            pl.BlockSpec((1, d), full),
            pl.BlockSpec((None, d, width), lambda j: (layer, 0, j)),
            pl.BlockSpec((1, width), full),
        ],
        out_specs=[pl.BlockSpec((n, width), full), pl.BlockSpec((n, width), full)],
        out_shape=[jax.ShapeDtypeStruct((n, width), BF16), jax.ShapeDtypeStruct((n, width), BF16)],
        scratch_shapes=[pltpu.VMEM((n, d), BF16)],
        compiler_params=_cparams(("arbitrary",)),
        name="memory_kv",
    )(mem2, gain, w_kv, k_gain)


def _mem_attn_kernel(q_ref, k_ref, v_ref, g_ref, o_ref, vx_scr):
    n_heads = q_ref.shape[2] // HEAD_DIM
    n_mem = v_ref.shape[1]
    for h in range(n_heads):
        vx_scr[h, :, :HEAD_DIM] = v_ref[0, :, h * HEAD_DIM:(h + 1) * HEAD_DIM]
        vx_scr[h, :, HEAD_DIM:] = jnp.ones((n_mem, HEAD_DIM), BF16)
    for h in range(n_heads):
        sl = slice(h * HEAD_DIM, (h + 1) * HEAD_DIM)
        s = lax.dot_general(q_ref[0, :, sl], k_ref[0, :, sl], (((1,), (1,)), ((), ())),
                            preferred_element_type=F32)
        m = jnp.max(s, axis=-1, keepdims=True)
        pb = jnp.exp2((s - m).astype(BF16))
        ol = jnp.dot(pb, vx_scr[h], preferred_element_type=F32)
        o = ol[:, :HEAD_DIM] * (1.0 / ol[:, HEAD_DIM:])
        o_ref[0, :, sl] = (_rms(o) * g_ref[:, sl]).astype(o_ref.dtype)


def _memory_attention(proj, q_col, k, v, gain, tm=1024):
    b, s, _ = proj.shape
    m, width = k.shape[1], k.shape[2]
    assert q_col % width == 0
    return pl.pallas_call(
        _mem_attn_kernel,
        grid=(b, s // tm),
        in_specs=[
            pl.BlockSpec((1, tm, width), lambda bi, t: (bi, t, q_col // width)),
            pl.BlockSpec((1, m, width), lambda bi, t: (bi, 0, 0)),
            pl.BlockSpec((1, m, width), lambda bi, t: (bi, 0, 0)),
            pl.BlockSpec((1, width), lambda bi, t: (0, 0)),
        ],
        out_specs=pl.BlockSpec((1, tm, width), lambda bi, t: (bi, t, 0)),
        out_shape=jax.ShapeDtypeStruct((b, s, width), BF16),
        scratch_shapes=[pltpu.VMEM((width // HEAD_DIM, m, 2 * HEAD_DIM), BF16)],
        compiler_params=_cparams(("parallel", "arbitrary")),
        name="memory_attention",
    )(proj, k, v, gain)


def _dft_matrices(seq):
    r1 = DFT_R1
    r2 = seq // r1
    sub = DFT_SUB
    k2 = np.arange(r2, dtype=np.int64)
    s2 = np.arange(r2, dtype=np.int64)
    s1 = np.arange(r1, dtype=np.int64)
    k1 = np.arange(r1, dtype=np.int64)
    th = 2.0 * np.pi * ((k2[:, None] * s2[None, :]) % r2).astype(np.float64) / r2
    fa = np.stack([np.cos(th), -np.sin(th)], axis=0)
    ma = np.einsum("cks,ab->ckasb", fa, np.eye(sub))
    ma = jnp.asarray(ma.reshape(2 * r2 * sub, r2 * sub), BF16)
    tw = 2.0 * np.pi * ((s1[:, None] * k2[None, :]) % seq).astype(np.float64) / seq
    tw = tw.reshape(r1 // sub, sub, r2).transpose(0, 2, 1).reshape(r1 // sub, r2 * sub, 1)
    tw_cos = jnp.broadcast_to(jnp.asarray(np.cos(tw), F32), (r1 // sub, r2 * sub, HEAD_DIM))
    tw_sin = jnp.broadcast_to(jnp.asarray(np.sin(tw), F32), (r1 // sub, r2 * sub, HEAD_DIM))
    ph = 2.0 * np.pi * ((k1[:, None] * s1[None, :]) % r1).astype(np.float64) / r1
    g = np.stack([np.cos(ph), np.sin(ph)], 0)
    mc = np.einsum("cks,ab->kacbs", g, np.eye(sub))
    mc = jnp.asarray(mc.reshape(r1 * sub, 2 * sub * r1), BF16)
    cc = np.arange(HEAD_DIM, dtype=np.int64)
    pc = 2.0 * np.pi * ((cc[:, None] * cc[None, :]) % HEAD_DIM).astype(np.float64) / HEAD_DIM
    mch = np.concatenate([np.cos(pc), -np.sin(pc)], axis=0) / np.sqrt(float(seq) * HEAD_DIM)
    return ma, tw_cos, tw_sin, mc, jnp.asarray(mch, BF16)


def _pack_bf16_pair(lo, hi):
    def bf16_bits(x):
        u = lax.bitcast_convert_type(x, jnp.uint32)
        return (u + (jnp.uint32(0x7FFF) + ((u >> 16) & jnp.uint32(1)))) >> 16
    return bf16_bits(lo) | (bf16_bits(hi) << 16)


def _unpack_bf16_pair(p):
    lo = lax.bitcast_convert_type(p << 16, F32)
    hi = lax.bitcast_convert_type(p & jnp.uint32(0xFFFF0000), F32)
    return lo, hi


def _fft_a_kernel(u_ref, ma_ref, twc_ref, tws_ref, mch_ref, wf_ref, t_ref):
    r2, sub, width = u_ref.shape[1], u_ref.shape[2], u_ref.shape[3]
    xb = u_ref[0].reshape(r2 * sub, width).astype(BF16)
    res = jnp.dot(ma_ref[...], xb, preferred_element_type=F32)
    half = r2 * sub
    tw_cos, tw_sin = twc_ref[0], tws_ref[0]
    for g in range(width // HEAD_DIM):
        sl = slice(g * HEAD_DIM, (g + 1) * HEAD_DIM)
        re, im = res[:half, sl], res[half:, sl]
        t_re = re * tw_cos + im * tw_sin
        t_im = im * tw_cos - re * tw_sin
        gg = jnp.dot(mch_ref[...], wf_ref[g], preferred_element_type=F32)
        g_re, g_im = gg[:HEAD_DIM], gg[HEAD_DIM:]
        blk = jnp.concatenate([jnp.concatenate([g_re, g_im], axis=1),
                               jnp.concatenate([-g_im, g_re], axis=1)], axis=0).astype(BF16)
        t2 = jnp.dot(jnp.concatenate([t_re, t_im], axis=1).astype(BF16), blk,
                     preferred_element_type=F32)
        packed = _pack_bf16_pair(t2[:, :HEAD_DIM], t2[:, HEAD_DIM:])
        t_ref[0, :, :, sl] = packed.reshape(r2, sub, HEAD_DIM)


def _fft_stage_a(u4, ma, tw_cos, tw_sin, mch, w_f, layer):
    b, r2, r1, width = u4.shape
    sub = DFT_SUB
    groups = width // HEAD_DIM
    tw_spec = pl.BlockSpec((1, r2 * sub, HEAD_DIM), lambda t, bi: (t, 0, 0))
    return pl.pallas_call(
        _fft_a_kernel,
        grid=(r1 // sub, b),
        in_specs=[
            pl.BlockSpec((1, r2, sub, width), lambda t, bi: (bi, 0, t, 0)),
            pl.BlockSpec(ma.shape, lambda t, bi: (0, 0)),
            tw_spec, tw_spec,
            pl.BlockSpec(mch.shape, lambda t, bi: (0, 0)),
            pl.BlockSpec((None, groups, HEAD_DIM, HEAD_DIM), lambda t, bi: (layer, 0, 0, 0)),
        ],
        out_specs=pl.BlockSpec((1, r2, sub, width), lambda t, bi: (bi, 0, t, 0)),
        out_shape=jax.ShapeDtypeStruct((b, r2, r1, width), jnp.uint32),
        compiler_params=_cparams(("arbitrary", "arbitrary")),
        name="fourier_stage_a",
    )(u4, ma, tw_cos, tw_sin, mch, w_f)


def _fft_c_kernel(t_ref, mc_ref, g_ref, o_ref):
    sub = DFT_SUB
    r1, width = t_ref.shape[2], t_ref.shape[3]
    for j in range(t_ref.shape[1] // sub):
        ks = slice(j * sub, (j + 1) * sub)
        tp = t_ref[0, ks].reshape(sub * r1, width)
        tb = jnp.concatenate(_unpack_bf16_pair(tp), axis=0).astype(BF16)
        res = jnp.dot(mc_ref[...], tb, preferred_element_type=F32)
        for g in range(width // HEAD_DIM):
            sl = slice(g * HEAD_DIM, (g + 1) * HEAD_DIM)
            z = _rms(res[:, sl]) * g_ref[:, sl]
            o_ref[0, :, ks, sl] = z.reshape(r1, sub, HEAD_DIM)


def _fft_stage_c(t4, mc, gain, k2_tiles=2):
    b, r2, r1, width = t4.shape
    sub = DFT_SUB * k2_tiles
    return pl.pallas_call(
        _fft_c_kernel,
        grid=(r2 // sub, b),
        in_specs=[
            pl.BlockSpec((1, sub, r1, width), lambda t, bi: (bi, t, 0, 0)),
            pl.BlockSpec(mc.shape, lambda t, bi: (0, 0)),
            pl.BlockSpec((1, width), lambda t, bi: (0, 0)),
        ],
        out_specs=pl.BlockSpec((1, r1, sub, width), lambda t, bi: (bi, 0, t, 0)),
        out_shape=jax.ShapeDtypeStruct((b, r1, r2, width), F32),
        compiler_params=_cparams(("arbitrary", "arbitrary")),
        name="fourier_stage_c",
    )(t4, mc, gain)


def _outproj_kernel(x_ref, ya_ref, yf_ref, ym_ref, w_ref, o_ref):
    na = ya_ref.shape[1]
    nf = yf_ref.shape[1]
    acc = x_ref[...]
    acc = acc + jnp.dot(ya_ref[...], w_ref[:na, :], preferred_element_type=F32)
    acc = acc + jnp.dot(yf_ref[...].astype(BF16), w_ref[na:na + nf, :], preferred_element_type=F32)
    acc = acc + jnp.dot(ym_ref[...], w_ref[na + nf:, :], preferred_element_type=F32)
    o_ref[...] = acc


def _out_projection(x2, y_na, y_f, y_m, w_out, layer, tm=512):
    n, d = x2.shape
    row = lambda i: (i, 0)
    return pl.pallas_call(
        _outproj_kernel,
        grid=(n // tm,),
        in_specs=[
            pl.BlockSpec((tm, d), row),
            pl.BlockSpec((tm, y_na.shape[1]), row),
            pl.BlockSpec((tm, y_f.shape[1]), row),
            pl.BlockSpec((tm, y_m.shape[1]), row),
            pl.BlockSpec((None,) + w_out.shape[1:], lambda i: (layer, 0, 0)),
        ],
        out_specs=pl.BlockSpec((tm, d), row),
        out_shape=jax.ShapeDtypeStruct((n, d), F32),
        compiler_params=_cparams(("parallel",)),
        name="out_projection",
    )(x2, y_na, y_f, y_m, w_out)


def _ffn_kernel(x_ref, g_ref, wg_ref, wu_ref, wd_ref, o_ref, h_scr):
    f = pl.program_id(1)

    def delta():
        h = h_scr[...]
        gate = jnp.dot(h, wg_ref[...], preferred_element_type=F32)
        up = jnp.dot(h, wu_ref[...], preferred_element_type=F32)
        act = (gate * jax.nn.sigmoid(gate) * up).astype(BF16)
        return jnp.dot(act, wd_ref[...], preferred_element_type=F32)

    @pl.when(f == 0)
    def _():
        x = x_ref[...]
        h_scr[...] = (_rms(x) * g_ref[...]).astype(BF16)
        o_ref[...] = x_ref[...] + delta()

    @pl.when(f != 0)
    def _():
        o_ref[...] += delta()


def _ffn(x2, gain, w_gate, w_up, w_down, layer, tm=1024, tf=512):
    n, d = x2.shape
    d_ff = w_gate.shape[2]
    row = lambda i, f: (i, 0)
    return pl.pallas_call(
        _ffn_kernel,
        grid=(n // tm, d_ff // tf),
        in_specs=[
            pl.BlockSpec((tm, d), row),
            pl.BlockSpec((1, d), lambda i, f: (0, 0)),
            pl.BlockSpec((None, d, tf), lambda i, f: (layer, 0, f)),
            pl.BlockSpec((None, d, tf), lambda i, f: (layer, 0, f)),
            pl.BlockSpec((None, tf, d), lambda i, f: (layer, f, 0)),
        ],
        out_specs=pl.BlockSpec((tm, d), row),
        out_shape=jax.ShapeDtypeStruct((n, d), F32),
        scratch_shapes=[pltpu.VMEM((tm, d), BF16)],
        compiler_params=_cparams(("parallel", "arbitrary"), vmem=56 * MIB),
        name="swiglu_ffn",
    )(x2, gain, w_gate, w_up, w_down)


def kernel(x, mem, attn_norm, w_in, na_q_norm, na_k_norm, na_rpb, w_fourier, mem_norm, w_mem_kv,
           mem_q_norm, mem_k_norm, out_norm, w_out, ffn_norm, w_gate, w_up, w_down):
    b, s, d = x.shape
    m = mem.shape[1]
    depth = w_in.shape[0]
    n_mem_heads = w_mem_kv.shape[2] // 2 // HEAD_DIM
    fourier_width = w_fourier.shape[1] * HEAD_DIM
    mem_width = n_mem_heads * HEAD_DIM
    na_width = (w_in.shape[2] - fourier_width - mem_width) // 3
    na_heads = na_width // HEAD_DIM
    q_scale = HEAD_DIM ** -0.5 * LOG2E

    ma, tw_cos, tw_sin, mc, mch = _dft_matrices(s)
    x2 = x.reshape(b * s, d)
    mem2 = mem.reshape(b * m, d)
    w_in_b, w_kv_b, w_f_b, w_out_b = (w.astype(BF16) for w in (w_in, w_mem_kv, w_fourier, w_out))
    w_gate_b, w_up_b, w_down_b = (w.astype(BF16) for w in (w_gate, w_up, w_down))
    first_mem_head = (3 * na_width + fourier_width) // HEAD_DIM
    normed_heads = frozenset(range(2 * na_heads)) | frozenset(
        range(first_mem_head, first_mem_head + n_mem_heads))

    for l in range(depth):
        head_gain = jnp.concatenate([
            jnp.tile(na_q_norm[l] * q_scale, na_heads),
            jnp.tile(na_k_norm[l], na_heads),
            jnp.ones((na_width + fourier_width,), F32),
            jnp.tile(mem_q_norm[l] * q_scale, n_mem_heads),
        ]).reshape(1, -1)
        proj, u_f = _in_projection(x2, attn_norm[l].reshape(1, d), w_in_b, head_gain, normed_heads, l,
                                   u_col=3 * na_width, u_width=fourier_width)
        proj = proj.reshape(b, s, -1)

        gain_out = out_norm[l].reshape(1, -1)
        y_na = _neighbourhood_attention(proj, _na_bias_blocks(na_rpb[l]), gain_out[:, :na_width], na_heads)

        k_m, v_m = _memory_kv(mem2, mem_norm[l].reshape(1, d), w_kv_b,
                              jnp.tile(mem_k_norm[l], n_mem_heads).reshape(1, -1), l)
        y_m = _memory_attention(proj, 3 * na_width + fourier_width, k_m.reshape(b, m, mem_width),
                                v_m.reshape(b, m, mem_width), gain_out[:, na_width + fourier_width:])

        t5 = _fft_stage_a(u_f.reshape(b, s // DFT_R1, DFT_R1, fourier_width), ma, tw_cos, tw_sin,
                          mch, w_f_b, l)
        y_f = _fft_stage_c(t5, mc, gain_out[:, na_width:na_width + fourier_width])

        x2 = _out_projection(x2, y_na.reshape(b * s, na_width), y_f.reshape(b * s, fourier_width),
                             y_m.reshape(b * s, mem_width), w_out_b, l)
        x2 = _ffn(x2, ffn_norm[l].reshape(1, d), w_gate_b, w_up_b, w_down_b, l)
    return x2.reshape(b, s, d)
```

```python
import functools

import numpy as np
import jax
import jax.numpy as jnp
from jax import lax
from jax.experimental import pallas as pl
from jax.experimental.pallas import tpu as pltpu

F32 = jnp.float32
BF16 = jnp.bfloat16

EPS = 1e-6
HEAD_DIM = 128
MXU_N = 256
GRID_W = 64
NA_KH = 8
NA_KW = 16
NA_ROWS_PER_BLOCK = 8
NA_HALO_ROWS = 4
NA_HEADS_PER_STEP = 4
MASK_VALUE = -1e30
NA_INVALID_ROW = 2 * NA_KH - 1
LOG2E = 1.4426950408889634

DFT_R1 = 64
DFT_SUB = 8
DFT_A_CHUNKS = 2

MIB = 1024 * 1024
VMEM_LIMIT = 48 * MIB


def _cparams(sem, vmem=VMEM_LIMIT):
    return pltpu.CompilerParams(dimension_semantics=sem, vmem_limit_bytes=vmem)


def _rms(x, eps=EPS):
    return x * lax.rsqrt(jnp.mean(x * x, axis=-1, keepdims=True) + eps)


def _head_norm_store(dst_ref, acc, gain, n_heads, col0=0):
    for h in range(n_heads):
        sl = slice(col0 + h * HEAD_DIM, col0 + (h + 1) * HEAD_DIM)
        dl = slice(h * HEAD_DIM, (h + 1) * HEAD_DIM)
        dst_ref[:, dl] = (_rms(acc[:, sl]) * gain[:, sl]).astype(dst_ref.dtype)


def _inproj_kernel(x_ref, g_ref, w_ref, hg_ref, p_ref, u_ref, *, normed_heads, u_col, row_chunks):
    rows = x_ref.shape[0] // row_chunks
    u_width = u_ref.shape[1]
    heads_per_chunk = MXU_N // HEAD_DIM
    chunks = sorted(range(w_ref.shape[1] // MXU_N),
                    key=lambda c: not any(c * heads_per_chunk + hh in normed_heads
                                          for hh in range(heads_per_chunk)))
    for r in range(row_chunks):
        rs = slice(r * rows, (r + 1) * rows)
        h = (_rms(x_ref[rs, :]) * g_ref[...]).astype(BF16)
        for c in chunks:
            acc = jnp.dot(h, w_ref[:, c * MXU_N:(c + 1) * MXU_N], preferred_element_type=F32)
            for hh in range(heads_per_chunk):
                blk = acc[:, hh * HEAD_DIM:(hh + 1) * HEAD_DIM]
                head = c * heads_per_chunk + hh
                sl = slice(head * HEAD_DIM, (head + 1) * HEAD_DIM)
                if head in normed_heads:
                    p_ref[rs, sl] = (_rms(blk) * hg_ref[:, sl]).astype(p_ref.dtype)
                else:
                    p_ref[rs, sl] = blk.astype(p_ref.dtype)
                if u_col <= sl.start < u_col + u_width:
                    u_ref[rs, sl.start - u_col:sl.stop - u_col] = blk


def _in_projection(x2, gain, w_in, head_gain, normed_heads, layer, u_col, u_width, tm=512, row_chunks=2):
    n, d = x2.shape
    d_in = w_in.shape[2]
    return pl.pallas_call(
        functools.partial(_inproj_kernel, normed_heads=normed_heads, u_col=u_col, row_chunks=row_chunks),
        grid=(n // tm,),
        in_specs=[
            pl.BlockSpec((tm, d), lambda i: (i, 0)),
            pl.BlockSpec((1, d), lambda i: (0, 0)),
            pl.BlockSpec((None, d, d_in), lambda i: (layer, 0, 0)),
            pl.BlockSpec((1, d_in), lambda i: (0, 0)),
        ],
        out_specs=[
            pl.BlockSpec((tm, d_in), lambda i: (i, 0)),
            pl.BlockSpec((tm, u_width), lambda i: (i, 0)),
        ],
        out_shape=[
            jax.ShapeDtypeStruct((n, d_in), BF16),
            jax.ShapeDtypeStruct((n, u_width), F32),
        ],
        compiler_params=_cparams(("parallel",)),
        name="in_projection",
    )(x2, gain, w_in, head_gain)


def _na_build_bias(variant, e_ref, bias_scr, hh):
    rq, halo = NA_ROWS_PER_BLOCK, NA_HALO_ROWS
    half_rows = rq // 2
    n_pairs = (half_rows + NA_KH) // 2
    lane = lax.broadcasted_iota(jnp.int32, (GRID_W, 2 * GRID_W), 1)
    for il in range(rq):
        jl0 = jnp.where(variant == 0, max(il, halo), jnp.where(variant == 2, min(il, halo), il))
        half = il // half_rows
        rows = slice((il % half_rows) * GRID_W, (il % half_rows + 1) * GRID_W)
        for pair in range(n_pairs):
            blocks = []
            for jl in (half * halo + 2 * pair, half * halo + 2 * pair + 1):
                inside = jnp.logical_and(jl >= jl0, jl < jl0 + NA_KH)
                idx = jnp.where(inside, jl - il - halo + NA_KH - 1, NA_INVALID_ROW)
                blocks.append(e_ref[hh, idx])
            bias_scr[hh, half, rows, pair * 2 * GRID_W:(pair + 1) * 2 * GRID_W] = (
                jnp.where(lane < GRID_W, blocks[0], blocks[1]))


def _na_kernel(q_ref, kp_ref, kc_ref, kn_ref, vp_ref, vc_ref, vn_ref, e_ref, g_ref, o_ref,
               bias_scr, vx_scr):
    n_batch = q_ref.shape[0]
    n_heads = q_ref.shape[2] // HEAD_DIM
    rb = pl.program_id(1)
    last = pl.num_programs(1) - 1
    half_tokens = q_ref.shape[1] // 2
    halo_tokens = kp_ref.shape[1]
    win_tokens = half_tokens + 2 * halo_tokens
    contract_last = (((1,), (1,)), ((), ()))

    @pl.when(jnp.logical_or(rb <= 1, rb == last))
    def _():
        variant = jnp.where(rb == 0, 0, jnp.where(rb == last, 2, 1))
        for hh in range(n_heads):
            _na_build_bias(variant, e_ref, bias_scr, hh)
        vx_scr[:, :, :, HEAD_DIM:] = jnp.ones(vx_scr.shape[:3] + (HEAD_DIM,), BF16)

    k_segments = ((kp_ref, kc_ref), (kc_ref, kn_ref))
    for hh in range(n_heads):
        hs = slice(hh * HEAD_DIM, (hh + 1) * HEAD_DIM)
        for b in range(n_batch):
            row0 = 0
            for v_ref in (vp_ref, vc_ref, vn_ref):
                vx_scr[b, hh, row0:row0 + v_ref.shape[1], :HEAD_DIM] = v_ref[b, :, hs]
                row0 += v_ref.shape[1]
            for half in range(2):
                rows = slice(half * half_tokens, (half + 1) * half_tokens)
                keys = slice(half * halo_tokens, half * halo_tokens + win_tokens)
                q = q_ref[b, rows, hs]
                s = jnp.concatenate(
                    [lax.dot_general(q, k_ref[b, :, hs], contract_last, preferred_element_type=F32)
                     for k_ref in k_segments[half]], axis=-1)
                s = s + bias_scr[hh, half]
                m = jnp.max(s, axis=-1, keepdims=True)
                pb = jnp.exp2((s - m).astype(BF16))
                ol = jnp.dot(pb, vx_scr[b, hh, keys, :], preferred_element_type=F32)
                o = ol[:, :HEAD_DIM] * (1.0 / ol[:, HEAD_DIM:])
                o_ref[b, rows, hs] = (_rms(o) * g_ref[:, hs]).astype(o_ref.dtype)


def _na_bias_blocks(rpb):
    n_heads = rpb.shape[0]
    kw = min(NA_KW, GRID_W)
    col = np.arange(GRID_W)
    col_start = np.clip(col - kw // 2, 0, GRID_W - kw)
    col_ok = (col[None, :] >= col_start[:, None]) & (col[None, :] < col_start[:, None] + kw)
    col_idx = np.clip(col[None, :] - col[:, None] + NA_KW - 1, 0, 2 * NA_KW - 2)
    onehot = jnp.asarray(col_idx[None] == np.arange(2 * NA_KW - 1)[:, None, None], F32)
    picked = jnp.einsum("hde,eqk->hdqk", rpb, onehot, precision=lax.Precision.HIGHEST)
    toe = jnp.where(col_ok[None, None], picked * LOG2E, MASK_VALUE)
    toe = jnp.concatenate([toe, jnp.full((n_heads, 1, GRID_W, GRID_W), MASK_VALUE, F32)], axis=1)
    return jnp.concatenate([toe, toe], axis=-1)


def _neighbourhood_attention(proj, bias_blocks, gain, n_heads):
    b, s, _ = proj.shape
    width = n_heads * HEAD_DIM
    tq = NA_ROWS_PER_BLOCK * GRID_W
    th = NA_HALO_ROWS * GRID_W
    n_blocks = s // tq
    n_halo_blocks = s // th
    per = tq // th
    assert n_blocks >= 3 and NA_KH == 2 * NA_HALO_ROWS == NA_ROWS_PER_BLOCK

    hps = NA_HEADS_PER_STEP
    wb = hps * HEAD_DIM
    assert n_heads % hps == 0

    def cur(col0):
        return pl.BlockSpec((b, tq, wb), lambda h, rb: (0, rb, col0 + h))

    def prev(col0):
        return pl.BlockSpec((b, th, wb), lambda h, rb: (0, jnp.maximum(per * rb - 1, 0), col0 + h))

    def nxt(col0):
        return pl.BlockSpec((b, th, wb),
                            lambda h, rb: (0, jnp.minimum(per * rb + per, n_halo_blocks - 1), col0 + h))

    k0, v0 = n_heads // hps, 2 * n_heads // hps
    return pl.pallas_call(
        _na_kernel,
        grid=(n_heads // hps, n_blocks),
        in_specs=[
            cur(0), prev(k0), cur(k0), nxt(k0), prev(v0), cur(v0), nxt(v0),
            pl.BlockSpec((hps,) + bias_blocks.shape[1:], lambda h, rb: (h, 0, 0, 0)),
            pl.BlockSpec((1, wb), lambda h, rb: (0, h)),
        ],
        out_specs=cur(0),
        out_shape=jax.ShapeDtypeStruct((b, s, width), BF16),
        scratch_shapes=[pltpu.VMEM((hps, 2, tq // 2, tq // 2 + 2 * th), F32),
                        pltpu.VMEM((b, hps, tq + 2 * th, 2 * HEAD_DIM), BF16)],
        compiler_params=_cparams(("arbitrary", "arbitrary"), vmem=52 * MIB),
        name="neighbourhood_attention",
    )(proj, proj, proj, proj, proj, proj, proj, bias_blocks, gain)


def _mem_kv_kernel(m_ref, g_ref, w_ref, kg_ref, k_ref, v_ref, h_scr):
    j = pl.program_id(0)

    @pl.when(j == 0)
    def _():
        h_scr[...] = (_rms(m_ref[...]) * g_ref[...]).astype(BF16)

    acc = jnp.dot(h_scr[...], w_ref[...], preferred_element_type=F32)

    @pl.when(j == 0)
    def _():
        _head_norm_store(k_ref, acc, kg_ref[...], acc.shape[1] // HEAD_DIM)

    @pl.when(j == 1)
    def _():
        v_ref[...] = acc.astype(v_ref.dtype)


def _memory_kv(mem2, gain, w_kv, k_gain, layer):
    n, d = mem2.shape
    width = w_kv.shape[2] // 2
    full = lambda j: (0, 0)
    return pl.pallas_call(
        _mem_kv_kernel,
        grid=(2,),
        in_specs=[
            pl.BlockSpec((n, d), full),
            pl.BlockSpec((1, d), full),
            pl.BlockSpec((None, d, width), lambda j: (layer, 0, j)),
            pl.BlockSpec((1, width), full),
        ],
        out_specs=[pl.BlockSpec((n, width), full), pl.BlockSpec((n, width), full)],
        out_shape=[jax.ShapeDtypeStruct((n, width), BF16), jax.ShapeDtypeStruct((n, width), BF16)],
        scratch_shapes=[pltpu.VMEM((n, d), BF16)],
        compiler_params=_cparams(("arbitrary",)),
        name="memory_kv",
    )(mem2, gain, w_kv, k_gain)


def _mem_attn_kernel(q_ref, k_ref, v_ref, g_ref, o_ref, vx_scr):
    n_heads = q_ref.shape[2] // HEAD_DIM
    n_mem = v_ref.shape[1]
    for h in range(n_heads):
        vx_scr[h, :, :HEAD_DIM] = v_ref[0, :, h * HEAD_DIM:(h + 1) * HEAD_DIM]
        vx_scr[h, :, HEAD_DIM:] = jnp.ones((n_mem, HEAD_DIM), BF16)
    for h in range(n_heads):
        sl = slice(h * HEAD_DIM, (h + 1) * HEAD_DIM)
        s = lax.dot_general(q_ref[0, :, sl], k_ref[0, :, sl], (((1,), (1,)), ((), ())),
                            preferred_element_type=F32)
        m = jnp.max(s, axis=-1, keepdims=True)
        pb = jnp.exp2((s - m).astype(BF16))
        ol = jnp.dot(pb, vx_scr[h], preferred_element_type=F32)
        o = ol[:, :HEAD_DIM] * (1.0 / ol[:, HEAD_DIM:])
        o_ref[0, :, sl] = (_rms(o) * g_ref[:, sl]).astype(o_ref.dtype)


def _memory_attention(proj, q_col, k, v, gain, tm=1024):
    b, s, _ = proj.shape
    m, width = k.shape[1], k.shape[2]
    assert q_col % width == 0
    return pl.pallas_call(
        _mem_attn_kernel,
        grid=(b, s // tm),
        in_specs=[
            pl.BlockSpec((1, tm, width), lambda bi, t: (bi, t, q_col // width)),
            pl.BlockSpec((1, m, width), lambda bi, t: (bi, 0, 0)),
            pl.BlockSpec((1, m, width), lambda bi, t: (bi, 0, 0)),
            pl.BlockSpec((1, width), lambda bi, t: (0, 0)),
        ],
        out_specs=pl.BlockSpec((1, tm, width), lambda bi, t: (bi, t, 0)),
        out_shape=jax.ShapeDtypeStruct((b, s, width), BF16),
        scratch_shapes=[pltpu.VMEM((width // HEAD_DIM, m, 2 * HEAD_DIM), BF16)],
        compiler_params=_cparams(("parallel", "arbitrary")),
        name="memory_attention",
    )(proj, k, v, gain)


def _dft_matrices(seq):
    r1 = DFT_R1
    r2 = seq // r1
    sub = DFT_SUB
    k2 = np.arange(r2, dtype=np.int64)
    s2 = np.arange(r2, dtype=np.int64)
    s1 = np.arange(r1, dtype=np.int64)
    k1 = np.arange(r1, dtype=np.int64)
    th = 2.0 * np.pi * ((k2[:, None] * s2[None, :]) % r2).astype(np.float64) / r2
    fa = np.stack([np.cos(th), -np.sin(th)], axis=0)
    ma = np.einsum("cks,ab->ckasb", fa, np.eye(sub))
    ma = ma.reshape(2, DFT_A_CHUNKS, r2 // DFT_A_CHUNKS, sub, r2 * sub).transpose(1, 0, 2, 3, 4)
    ma = ma.reshape(DFT_A_CHUNKS, 2 * r2 * sub // DFT_A_CHUNKS, r2 * sub)
    tw = 2.0 * np.pi * ((s1[:, None] * k2[None, :]) % seq).astype(np.float64) / seq
    tw = tw.reshape(r1 // sub, sub, r2).transpose(0, 2, 1).reshape(r1 // sub, r2 * sub, 1)
    tw_cos = jnp.broadcast_to(jnp.asarray(np.cos(tw), F32), (r1 // sub, r2 * sub, HEAD_DIM))
    tw_sin = jnp.broadcast_to(jnp.asarray(np.sin(tw), F32), (r1 // sub, r2 * sub, HEAD_DIM))
    ph = 2.0 * np.pi * ((k1[:, None] * s1[None, :]) % r1).astype(np.float64) / r1
    g = np.stack([np.cos(ph), np.sin(ph)], 0)
    mc = np.einsum("cks,ab->kacbs", g, np.eye(sub))
    mc = mc.reshape(r1 * sub, 2 * sub * r1)
    cc = np.arange(HEAD_DIM, dtype=np.int64)
    pc = 2.0 * np.pi * ((cc[:, None] * cc[None, :]) % HEAD_DIM).astype(np.float64) / HEAD_DIM
    mch = np.concatenate([np.cos(pc), -np.sin(pc)], axis=0) / np.sqrt(float(seq) * HEAD_DIM)
    ma, mc, mch = (jnp.asarray(m, F32).astype(BF16) for m in (ma, mc, mch))
    return ma, tw_cos, tw_sin, mc, mch


def _fft_a_kernel(u_ref, ma_ref, twc_ref, tws_ref, mch_ref, wf_ref, t_ref):
    r2, sub, width = u_ref.shape[1], u_ref.shape[2], u_ref.shape[3]
    xb = u_ref[0].reshape(r2 * sub, width).astype(BF16)
    n_chunks = ma_ref.shape[0]
    k2c = r2 // n_chunks
    half = k2c * sub
    blocks = []
    for g in range(width // HEAD_DIM):
        gg = jnp.dot(mch_ref[...], wf_ref[g], preferred_element_type=F32)
        g_re, g_im = gg[:HEAD_DIM], gg[HEAD_DIM:]
        blocks.append(jnp.concatenate([jnp.concatenate([g_re, g_im], axis=1),
                                       jnp.concatenate([-g_im, g_re], axis=1)], axis=0).astype(BF16))
    for ch in range(n_chunks):
        res = jnp.dot(ma_ref[ch], xb, preferred_element_type=F32)
        tw_cos = twc_ref[0, ch * half:(ch + 1) * half, :]
        tw_sin = tws_ref[0, ch * half:(ch + 1) * half, :]
        for g in range(width // HEAD_DIM):
            sl = slice(g * HEAD_DIM, (g + 1) * HEAD_DIM)
            re, im = res[:half, sl], res[half:, sl]
            t_re = re * tw_cos + im * tw_sin
            t_im = im * tw_cos - re * tw_sin
            t2 = jnp.dot(jnp.concatenate([t_re, t_im], axis=1).astype(BF16), blocks[g],
                         preferred_element_type=F32)
            ks = slice(ch * k2c, (ch + 1) * k2c)
            t_ref[0, 0, ks, :, sl] = t2[:, :HEAD_DIM].reshape(k2c, sub, HEAD_DIM)
            t_ref[0, 1, ks, :, sl] = t2[:, HEAD_DIM:].reshape(k2c, sub, HEAD_DIM)


def _fft_stage_a(u4, ma, tw_cos, tw_sin, mch, w_f, layer):
    b, r2, r1, width = u4.shape
    sub = DFT_SUB
    groups = width // HEAD_DIM
    tw_spec = pl.BlockSpec((1, r2 * sub, HEAD_DIM), lambda t, bi: (t, 0, 0))
    return pl.pallas_call(
        _fft_a_kernel,
        grid=(r1 // sub, b),
        in_specs=[
            pl.BlockSpec((1, r2, sub, width), lambda t, bi: (bi, 0, t, 0)),
            pl.BlockSpec(ma.shape, lambda t, bi: (0, 0, 0)),
            tw_spec, tw_spec,
            pl.BlockSpec(mch.shape, lambda t, bi: (0, 0)),
            pl.BlockSpec((None, groups, HEAD_DIM, HEAD_DIM), lambda t, bi: (layer, 0, 0, 0)),
        ],
        out_specs=pl.BlockSpec((1, 2, r2, sub, width), lambda t, bi: (bi, 0, 0, t, 0)),
        out_shape=jax.ShapeDtypeStruct((b, 2, r2, r1, width), F32),
        compiler_params=_cparams(("arbitrary", "arbitrary")),
        name="fourier_stage_a",
    )(u4, ma, tw_cos, tw_sin, mch, w_f)


def _fft_c_kernel(t_ref, mc_ref, g_ref, o_ref):
    sub = DFT_SUB
    r1, width = t_ref.shape[3], t_ref.shape[4]
    for j in range(t_ref.shape[2] // sub):
        ks = slice(j * sub, (j + 1) * sub)
        tb = jnp.concatenate([t_ref[0, c, ks].reshape(sub * r1, width) for c in range(2)],
                             axis=0).astype(BF16)
        res = jnp.dot(mc_ref[...], tb, preferred_element_type=F32)
        for g in range(width // HEAD_DIM):
            sl = slice(g * HEAD_DIM, (g + 1) * HEAD_DIM)
            z = _rms(res[:, sl]) * g_ref[:, sl]
            o_ref[0, :, ks, sl] = z.reshape(r1, sub, HEAD_DIM)


def _fft_stage_c(t5, mc, gain, k2_tiles=2):
    b, _, r2, r1, width = t5.shape
    sub = DFT_SUB * k2_tiles
    return pl.pallas_call(
        _fft_c_kernel,
        grid=(r2 // sub, b),
        in_specs=[
            pl.BlockSpec((1, 2, sub, r1, width), lambda t, bi: (bi, 0, t, 0, 0)),
            pl.BlockSpec(mc.shape, lambda t, bi: (0, 0)),
            pl.BlockSpec((1, width), lambda t, bi: (0, 0)),
        ],
        out_specs=pl.BlockSpec((1, r1, sub, width), lambda t, bi: (bi, 0, t, 0)),
        out_shape=jax.ShapeDtypeStruct((b, r1, r2, width), F32),
        compiler_params=_cparams(("arbitrary", "arbitrary")),
        name="fourier_stage_c",
    )(t5, mc, gain)


def _outproj_kernel(x_ref, ya_ref, yf_ref, ym_ref, w_ref, o_ref):
    na = ya_ref.shape[1]
    nf = yf_ref.shape[1]
    acc = x_ref[...]
    acc = acc + jnp.dot(ya_ref[...], w_ref[:na, :], preferred_element_type=F32)
    acc = acc + jnp.dot(yf_ref[...].astype(BF16), w_ref[na:na + nf, :], preferred_element_type=F32)
    acc = acc + jnp.dot(ym_ref[...], w_ref[na + nf:, :], preferred_element_type=F32)
    o_ref[...] = acc


def _out_projection(x2, y_na, y_f, y_m, w_out, layer, tm=512):
    n, d = x2.shape
    row = lambda i: (i, 0)
    return pl.pallas_call(
        _outproj_kernel,
        grid=(n // tm,),
        in_specs=[
            pl.BlockSpec((tm, d), row),
            pl.BlockSpec((tm, y_na.shape[1]), row),
            pl.BlockSpec((tm, y_f.shape[1]), row),
            pl.BlockSpec((tm, y_m.shape[1]), row),
            pl.BlockSpec((None,) + w_out.shape[1:], lambda i: (layer, 0, 0)),
        ],
        out_specs=pl.BlockSpec((tm, d), row),
        out_shape=jax.ShapeDtypeStruct((n, d), F32),
        compiler_params=_cparams(("parallel",)),
        name="out_projection",
    )(x2, y_na, y_f, y_m, w_out)


def _ffn_kernel(x_ref, g_ref, wg_ref, wu_ref, wd_ref, o_ref, h_scr):
    f = pl.program_id(1)

    def delta():
        h = h_scr[...]
        gate = jnp.dot(h, wg_ref[...], preferred_element_type=F32)
        up = jnp.dot(h, wu_ref[...], preferred_element_type=F32)
        act = (gate * jax.nn.sigmoid(gate) * up).astype(BF16)
        return jnp.dot(act, wd_ref[...], preferred_element_type=F32)

    @pl.when(f == 0)
    def _():
        x = x_ref[...]
        h_scr[...] = (_rms(x) * g_ref[...]).astype(BF16)
        o_ref[...] = x_ref[...] + delta()

    @pl.when(f != 0)
    def _():
        o_ref[...] += delta()


def _ffn(x2, gain, w_gate, w_up, w_down, layer, tm=1024, tf=512):
    n, d = x2.shape
    d_ff = w_gate.shape[2]
    row = lambda i, f: (i, 0)
    return pl.pallas_call(
        _ffn_kernel,
        grid=(n // tm, d_ff // tf),
        in_specs=[
            pl.BlockSpec((tm, d), row),
            pl.BlockSpec((1, d), lambda i, f: (0, 0)),
            pl.BlockSpec((None, d, tf), lambda i, f: (layer, 0, f)),
            pl.BlockSpec((None, d, tf), lambda i, f: (layer, 0, f)),
            pl.BlockSpec((None, tf, d), lambda i, f: (layer, f, 0)),
        ],
        out_specs=pl.BlockSpec((tm, d), row),
        out_shape=jax.ShapeDtypeStruct((n, d), F32),
        scratch_shapes=[pltpu.VMEM((tm, d), BF16)],
        compiler_params=_cparams(("parallel", "arbitrary"), vmem=56 * MIB),
        name="swiglu_ffn",
    )(x2, gain, w_gate, w_up, w_down)


def kernel(x, mem, attn_norm, w_in, na_q_norm, na_k_norm, na_rpb, w_fourier, mem_norm, w_mem_kv,
           mem_q_norm, mem_k_norm, out_norm, w_out, ffn_norm, w_gate, w_up, w_down):
    b, s, d = x.shape
    m = mem.shape[1]
    depth = w_in.shape[0]
    n_mem_heads = w_mem_kv.shape[2] // 2 // HEAD_DIM
    fourier_width = w_fourier.shape[1] * HEAD_DIM
    mem_width = n_mem_heads * HEAD_DIM
    na_width = (w_in.shape[2] - fourier_width - mem_width) // 3
    na_heads = na_width // HEAD_DIM
    q_scale = HEAD_DIM ** -0.5 * LOG2E

    ma, tw_cos, tw_sin, mc, mch = _dft_matrices(s)
    x2 = x.reshape(b * s, d)
    mem2 = mem.reshape(b * m, d)
    w_in_b, w_kv_b, w_f_b, w_out_b = (w.astype(BF16) for w in (w_in, w_mem_kv, w_fourier, w_out))
    w_gate_b, w_up_b, w_down_b = (w.astype(BF16) for w in (w_gate, w_up, w_down))
    first_mem_head = (3 * na_width + fourier_width) // HEAD_DIM
    normed_heads = frozenset(range(2 * na_heads)) | frozenset(
        range(first_mem_head, first_mem_head + n_mem_heads))

    for l in range(depth):
        head_gain = jnp.concatenate([
            jnp.tile(na_q_norm[l] * q_scale, na_heads),
            jnp.tile(na_k_norm[l], na_heads),
            jnp.ones((na_width + fourier_width,), F32),
            jnp.tile(mem_q_norm[l] * q_scale, n_mem_heads),
        ]).reshape(1, -1)
        proj, u_f = _in_projection(x2, attn_norm[l].reshape(1, d), w_in_b, head_gain, normed_heads, l,
                                   u_col=3 * na_width, u_width=fourier_width)
        proj = proj.reshape(b, s, -1)

        gain_out = out_norm[l].reshape(1, -1)
        y_na = _neighbourhood_attention(proj, _na_bias_blocks(na_rpb[l]), gain_out[:, :na_width], na_heads)

        k_m, v_m = _memory_kv(mem2, mem_norm[l].reshape(1, d), w_kv_b,
                              jnp.tile(mem_k_norm[l], n_mem_heads).reshape(1, -1), l)
        y_m = _memory_attention(proj, 3 * na_width + fourier_width, k_m.reshape(b, m, mem_width),
                                v_m.reshape(b, m, mem_width), gain_out[:, na_width + fourier_width:])

        t5 = _fft_stage_a(u_f.reshape(b, s // DFT_R1, DFT_R1, fourier_width), ma, tw_cos, tw_sin,
                          mch, w_f_b, l)
        y_f = _fft_stage_c(t5, mc, gain_out[:, na_width:na_width + fourier_width])

        x2 = _out_projection(x2, y_na.reshape(b * s, na_width), y_f.reshape(b * s, fourier_width),
                             y_m.reshape(b * s, mem_width), w_out_b, l)
        x2 = _ffn(x2, ffn_norm[l].reshape(1, d), w_gate_b, w_up_b, w_down_b, l)
    return x2.reshape(b, s, d)
```

```python
import functools

import numpy as np
import jax
import jax.numpy as jnp
from jax import lax
from jax.experimental import pallas as pl
from jax.experimental.pallas import tpu as pltpu

F32 = jnp.float32
BF16 = jnp.bfloat16

EPS = 1e-6
HEAD_DIM = 128
MXU_N = 256
GRID_W = 64
NA_KH = 8
NA_KW = 16
NA_ROWS_PER_BLOCK = 8
NA_HALO_ROWS = 4
NA_HEADS_PER_STEP = 4
MASK_VALUE = -1e30
NA_INVALID_ROW = 2 * NA_KH - 1
LOG2E = 1.4426950408889634

DFT_R1 = 64
DFT_SUB = 8
DFT_A_CHUNKS = 2

MIB = 1024 * 1024
VMEM_LIMIT = 48 * MIB


def _cparams(sem, vmem=VMEM_LIMIT):
    return pltpu.CompilerParams(dimension_semantics=sem, vmem_limit_bytes=vmem)


def _rms(x, eps=EPS):
    return x * lax.rsqrt(jnp.mean(x * x, axis=-1, keepdims=True) + eps)


def _head_norm_store(dst_ref, acc, gain, n_heads, col0=0):
    for h in range(n_heads):
        sl = slice(col0 + h * HEAD_DIM, col0 + (h + 1) * HEAD_DIM)
        dl = slice(h * HEAD_DIM, (h + 1) * HEAD_DIM)
        dst_ref[:, dl] = (_rms(acc[:, sl]) * gain[:, sl]).astype(dst_ref.dtype)


def _inproj_kernel(x_ref, g_ref, w_ref, hg_ref, p_ref, u_ref, *, normed_heads, u_col, row_chunks):
    rows = x_ref.shape[0] // row_chunks
    u_width = u_ref.shape[1]
    heads_per_chunk = MXU_N // HEAD_DIM
    chunks = sorted(range(w_ref.shape[1] // MXU_N),
                    key=lambda c: not any(c * heads_per_chunk + hh in normed_heads
                                          for hh in range(heads_per_chunk)))
    for r in range(row_chunks):
        rs = slice(r * rows, (r + 1) * rows)
        h = (_rms(x_ref[rs, :]) * g_ref[...]).astype(BF16)
        for c in chunks:
            acc = jnp.dot(h, w_ref[:, c * MXU_N:(c + 1) * MXU_N], preferred_element_type=F32)
            for hh in range(heads_per_chunk):
                blk = acc[:, hh * HEAD_DIM:(hh + 1) * HEAD_DIM]
                head = c * heads_per_chunk + hh
                sl = slice(head * HEAD_DIM, (head + 1) * HEAD_DIM)
                if head in normed_heads:
                    p_ref[rs, sl] = (_rms(blk) * hg_ref[:, sl]).astype(p_ref.dtype)
                else:
                    p_ref[rs, sl] = blk.astype(p_ref.dtype)
                if u_col <= sl.start < u_col + u_width:
                    u_ref[rs, sl.start - u_col:sl.stop - u_col] = blk


def _in_projection(x2, gain, w_in, head_gain, normed_heads, layer, u_col, u_width, tm=512, row_chunks=2):
    n, d = x2.shape
    d_in = w_in.shape[2]
    return pl.pallas_call(
        functools.partial(_inproj_kernel, normed_heads=normed_heads, u_col=u_col, row_chunks=row_chunks),
        grid=(n // tm,),
        in_specs=[
            pl.BlockSpec((tm, d), lambda i: (i, 0)),
            pl.BlockSpec((1, d), lambda i: (0, 0)),
            pl.BlockSpec((None, d, d_in), lambda i: (layer, 0, 0)),
            pl.BlockSpec((1, d_in), lambda i: (0, 0)),
        ],
        out_specs=[
            pl.BlockSpec((tm, d_in), lambda i: (i, 0)),
            pl.BlockSpec((tm, u_width), lambda i: (i, 0)),
        ],
        out_shape=[
            jax.ShapeDtypeStruct((n, d_in), BF16),
            jax.ShapeDtypeStruct((n, u_width), F32),
        ],
        compiler_params=_cparams(("parallel",)),
        name="in_projection",
    )(x2, gain, w_in, head_gain)


def _na_build_bias(variant, e_ref, bias_scr, hh):
    rq, halo = NA_ROWS_PER_BLOCK, NA_HALO_ROWS
    half_rows = rq // 2
    n_pairs = (half_rows + NA_KH) // 2
    lane = lax.broadcasted_iota(jnp.int32, (GRID_W, 2 * GRID_W), 1)
    for il in range(rq):
        jl0 = jnp.where(variant == 0, max(il, halo), jnp.where(variant == 2, min(il, halo), il))
        half = il // half_rows
        rows = slice((il % half_rows) * GRID_W, (il % half_rows + 1) * GRID_W)
        for pair in range(n_pairs):
            blocks = []
            for jl in (half * halo + 2 * pair, half * halo + 2 * pair + 1):
                inside = jnp.logical_and(jl >= jl0, jl < jl0 + NA_KH)
                idx = jnp.where(inside, jl - il - halo + NA_KH - 1, NA_INVALID_ROW)
                blocks.append(e_ref[hh, idx])
            bias_scr[hh, half, rows, pair * 2 * GRID_W:(pair + 1) * 2 * GRID_W] = (
                jnp.where(lane < GRID_W, blocks[0], blocks[1]))


def _na_kernel(q_ref, kp_ref, kc_ref, kn_ref, vp_ref, vc_ref, vn_ref, e_ref, g_ref, o_ref,
               bias_scr, vx_scr):
    n_batch = q_ref.shape[0]
    n_heads = q_ref.shape[2] // HEAD_DIM
    rb = pl.program_id(1)
    last = pl.num_programs(1) - 1
    half_tokens = q_ref.shape[1] // 2
    halo_tokens = kp_ref.shape[1]
    win_tokens = half_tokens + 2 * halo_tokens
    contract_last = (((1,), (1,)), ((), ()))

    @pl.when(jnp.logical_or(rb <= 1, rb == last))
    def _():
        variant = jnp.where(rb == 0, 0, jnp.where(rb == last, 2, 1))
        for hh in range(n_heads):
            _na_build_bias(variant, e_ref, bias_scr, hh)
        vx_scr[:, :, :, HEAD_DIM:] = jnp.ones(vx_scr.shape[:3] + (HEAD_DIM,), BF16)

    k_segments = ((kp_ref, kc_ref), (kc_ref, kn_ref))
    for hh in range(n_heads):
        hs = slice(hh * HEAD_DIM, (hh + 1) * HEAD_DIM)
        for b in range(n_batch):
            row0 = 0
            for v_ref in (vp_ref, vc_ref, vn_ref):
                vx_scr[b, hh, row0:row0 + v_ref.shape[1], :HEAD_DIM] = v_ref[b, :, hs]
                row0 += v_ref.shape[1]
            for half in range(2):
                rows = slice(half * half_tokens, (half + 1) * half_tokens)
                keys = slice(half * halo_tokens, half * halo_tokens + win_tokens)
                q = q_ref[b, rows, hs]
                s = jnp.concatenate(
                    [lax.dot_general(q, k_ref[b, :, hs], contract_last, preferred_element_type=F32)
                     for k_ref in k_segments[half]], axis=-1)
                s = s + bias_scr[hh, half]
                m = jnp.max(s, axis=-1, keepdims=True)
                pb = jnp.exp2((s - m).astype(BF16))
                ol = jnp.dot(pb, vx_scr[b, hh, keys, :], preferred_element_type=F32)
                o = ol[:, :HEAD_DIM] * (1.0 / ol[:, HEAD_DIM:])
                o_ref[b, rows, hs] = (_rms(o) * g_ref[:, hs]).astype(o_ref.dtype)


def _na_bias_blocks(rpb):
    n_heads = rpb.shape[0]
    kw = min(NA_KW, GRID_W)
    col = np.arange(GRID_W)
    col_start = np.clip(col - kw // 2, 0, GRID_W - kw)
    col_ok = (col[None, :] >= col_start[:, None]) & (col[None, :] < col_start[:, None] + kw)
    col_idx = np.clip(col[None, :] - col[:, None] + NA_KW - 1, 0, 2 * NA_KW - 2)
    onehot = jnp.asarray(col_idx[None] == np.arange(2 * NA_KW - 1)[:, None, None], F32)
    picked = jnp.einsum("hde,eqk->hdqk", rpb, onehot, precision=lax.Precision.HIGHEST)
    toe = jnp.where(col_ok[None, None], picked * LOG2E, MASK_VALUE)
    toe = jnp.concatenate([toe, jnp.full((n_heads, 1, GRID_W, GRID_W), MASK_VALUE, F32)], axis=1)
    return jnp.concatenate([toe, toe], axis=-1)


def _neighbourhood_attention(proj, bias_blocks, gain, n_heads):
    b, s, _ = proj.shape
    width = n_heads * HEAD_DIM
    tq = NA_ROWS_PER_BLOCK * GRID_W
    th = NA_HALO_ROWS * GRID_W
    n_blocks = s // tq
    n_halo_blocks = s // th
    per = tq // th
    assert n_blocks >= 3 and NA_KH == 2 * NA_HALO_ROWS == NA_ROWS_PER_BLOCK

    hps = NA_HEADS_PER_STEP
    wb = hps * HEAD_DIM
    assert n_heads % hps == 0

    def cur(col0):
        return pl.BlockSpec((b, tq, wb), lambda h, rb: (0, rb, col0 + h))

    def prev(col0):
        return pl.BlockSpec((b, th, wb), lambda h, rb: (0, jnp.maximum(per * rb - 1, 0), col0 + h))

    def nxt(col0):
        return pl.BlockSpec((b, th, wb),
                            lambda h, rb: (0, jnp.minimum(per * rb + per, n_halo_blocks - 1), col0 + h))

    k0, v0 = n_heads // hps, 2 * n_heads // hps
    return pl.pallas_call(
        _na_kernel,
        grid=(n_heads // hps, n_blocks),
        in_specs=[
            cur(0), prev(k0), cur(k0), nxt(k0), prev(v0), cur(v0), nxt(v0),
            pl.BlockSpec((hps,) + bias_blocks.shape[1:], lambda h, rb: (h, 0, 0, 0)),
            pl.BlockSpec((1, wb), lambda h, rb: (0, h)),
        ],
        out_specs=cur(0),
        out_shape=jax.ShapeDtypeStruct((b, s, width), BF16),
        scratch_shapes=[pltpu.VMEM((hps, 2, tq // 2, tq // 2 + 2 * th), F32),
                        pltpu.VMEM((b, hps, tq + 2 * th, 2 * HEAD_DIM), BF16)],
        compiler_params=_cparams(("arbitrary", "arbitrary"), vmem=52 * MIB),
        name="neighbourhood_attention",
    )(proj, proj, proj, proj, proj, proj, proj, bias_blocks, gain)


def _mem_kv_kernel(m_ref, g_ref, w_ref, kg_ref, k_ref, v_ref, h_scr):
    j = pl.program_id(0)

    @pl.when(j == 0)
    def _():
        h_scr[...] = (_rms(m_ref[...]) * g_ref[...]).astype(BF16)

    acc = jnp.dot(h_scr[...], w_ref[...], preferred_element_type=F32)

    @pl.when(j == 0)
    def _():
        _head_norm_store(k_ref, acc, kg_ref[...], acc.shape[1] // HEAD_DIM)

    @pl.when(j == 1)
    def _():
        v_ref[...] = acc.astype(v_ref.dtype)


def _memory_kv(mem2, gain, w_kv, k_gain, layer):
    n, d = mem2.shape
    width = w_kv.shape[2] // 2
    full = lambda j: (0, 0)
    return pl.pallas_call(
        _mem_kv_kernel,
        grid=(2,),
        in_specs=[
            pl.BlockSpec((n, d), full),
            pl.BlockSpec((1, d), full),
            pl.BlockSpec((None, d, width), lambda j: (layer, 0, j)),
            pl.BlockSpec((1, width), full),
        ],
        out_specs=[pl.BlockSpec((n, width), full), pl.BlockSpec((n, width), full)],
        out_shape=[jax.ShapeDtypeStruct((n, width), BF16), jax.ShapeDtypeStruct((n, width), BF16)],
        scratch_shapes=[pltpu.VMEM((n, d), BF16)],
        compiler_params=_cparams(("arbitrary",)),
        name="memory_kv",
    )(mem2, gain, w_kv, k_gain)


def _mem_attn_kernel(q_ref, k_ref, v_ref, g_ref, o_ref, vx_scr):
    n_heads = q_ref.shape[2] // HEAD_DIM
    n_mem = v_ref.shape[1]
    for h in range(n_heads):
        vx_scr[h, :, :HEAD_DIM] = v_ref[0, :, h * HEAD_DIM:(h + 1) * HEAD_DIM]
        vx_scr[h, :, HEAD_DIM:] = jnp.ones((n_mem, HEAD_DIM), BF16)
    for h in range(n_heads):
        sl = slice(h * HEAD_DIM, (h + 1) * HEAD_DIM)
        s = lax.dot_general(q_ref[0, :, sl], k_ref[0, :, sl], (((1,), (1,)), ((), ())),
                            preferred_element_type=F32)
        m = jnp.max(s, axis=-1, keepdims=True)
        pb = jnp.exp2((s - m).astype(BF16))
        ol = jnp.dot(pb, vx_scr[h], preferred_element_type=F32)
        o = ol[:, :HEAD_DIM] * (1.0 / ol[:, HEAD_DIM:])
        o_ref[0, :, sl] = (_rms(o) * g_ref[:, sl]).astype(o_ref.dtype)


def _memory_attention(proj, q_col, k, v, gain, tm=1024):
    b, s, _ = proj.shape
    m, width = k.shape[1], k.shape[2]
    assert q_col % width == 0
    return pl.pallas_call(
        _mem_attn_kernel,
        grid=(b, s // tm),
        in_specs=[
            pl.BlockSpec((1, tm, width), lambda bi, t: (bi, t, q_col // width)),
            pl.BlockSpec((1, m, width), lambda bi, t: (bi, 0, 0)),
            pl.BlockSpec((1, m, width), lambda bi, t: (bi, 0, 0)),
            pl.BlockSpec((1, width), lambda bi, t: (0, 0)),
        ],
        out_specs=pl.BlockSpec((1, tm, width), lambda bi, t: (bi, t, 0)),
        out_shape=jax.ShapeDtypeStruct((b, s, width), BF16),
        scratch_shapes=[pltpu.VMEM((width // HEAD_DIM, m, 2 * HEAD_DIM), BF16)],
        compiler_params=_cparams(("parallel", "arbitrary")),
        name="memory_attention",
    )(proj, k, v, gain)


def _dft_matrices(seq):
    r1 = DFT_R1
    r2 = seq // r1
    sub = DFT_SUB
    k2 = np.arange(r2, dtype=np.int64)
    s2 = np.arange(r2, dtype=np.int64)
    s1 = np.arange(r1, dtype=np.int64)
    k1 = np.arange(r1, dtype=np.int64)
    n_ch = DFT_A_CHUNKS
    th = 2.0 * np.pi * ((k2[:, None] * s2[None, :]) % r2).astype(np.float64) / r2
    fa = np.stack([np.cos(th), -np.sin(th)], axis=0)
    ma = np.einsum("cks,ab->ckasb", fa, np.eye(sub))
    ma = ma.reshape(2, n_ch, r2 // n_ch, sub, r2 * sub).transpose(1, 0, 3, 2, 4)
    ma = ma.reshape(n_ch, 2 * r2 * sub // n_ch, r2 * sub)
    tw = 2.0 * np.pi * ((s1[:, None] * k2[None, :]) % seq).astype(np.float64) / seq
    tw = tw.reshape(r1 // sub, sub, n_ch, r2 // n_ch).transpose(0, 2, 1, 3)
    tw = tw.reshape(r1 // sub, n_ch, r2 * sub // n_ch, 1)
    tw_shape = tw.shape[:3] + (HEAD_DIM,)
    tw_cos = jnp.broadcast_to(jnp.asarray(np.cos(tw), F32), tw_shape)
    tw_sin = jnp.broadcast_to(jnp.asarray(np.sin(tw), F32), tw_shape)
    ph = 2.0 * np.pi * ((k1[:, None] * s1[None, :]) % r1).astype(np.float64) / r1
    g = np.stack([np.cos(ph), np.sin(ph)], 0)
    mc = np.einsum("cks,ab->kacsb", g, np.eye(sub))
    mc = mc.reshape(r1 * sub, 2 * sub * r1)
    cc = np.arange(HEAD_DIM, dtype=np.int64)
    pc = 2.0 * np.pi * ((cc[:, None] * cc[None, :]) % HEAD_DIM).astype(np.float64) / HEAD_DIM
    mch = np.concatenate([np.cos(pc), -np.sin(pc)], axis=0) / np.sqrt(float(seq) * HEAD_DIM)
    ma, mc, mch = (jnp.asarray(m, F32).astype(BF16) for m in (ma, mc, mch))
    return ma, tw_cos, tw_sin, mc, mch


def _fft_a_kernel(u_ref, ma_ref, twc_ref, tws_ref, mch_ref, wf_ref, t_ref):
    r2, sub, width = u_ref.shape[1], u_ref.shape[2], u_ref.shape[3]
    xb = u_ref[0].reshape(r2 * sub, width).astype(BF16)
    n_chunks = ma_ref.shape[0]
    k2c = r2 // n_chunks
    half = k2c * sub
    blocks = []
    for g in range(width // HEAD_DIM):
        gg = jnp.dot(mch_ref[...], wf_ref[g], preferred_element_type=F32)
        g_re, g_im = gg[:HEAD_DIM], gg[HEAD_DIM:]
        blocks.append(jnp.concatenate([jnp.concatenate([g_re, g_im], axis=1),
                                       jnp.concatenate([-g_im, g_re], axis=1)], axis=0).astype(BF16))
    for ch in range(n_chunks):
        res = jnp.dot(ma_ref[ch], xb, preferred_element_type=F32)
        tw_cos, tw_sin = twc_ref[0, ch], tws_ref[0, ch]
        for g in range(width // HEAD_DIM):
            sl = slice(g * HEAD_DIM, (g + 1) * HEAD_DIM)
            re, im = res[:half, sl], res[half:, sl]
            t_re = re * tw_cos + im * tw_sin
            t_im = im * tw_cos - re * tw_sin
            t2 = jnp.dot(jnp.concatenate([t_re, t_im], axis=1).astype(BF16), blocks[g],
                         preferred_element_type=F32)
            ks = slice(ch * k2c, (ch + 1) * k2c)
            for c in range(2):
                part = t2[:, c * HEAD_DIM:(c + 1) * HEAD_DIM]
                t_ref[0, c, :, ks, sl] = part.reshape(sub, k2c, HEAD_DIM).astype(t_ref.dtype)


def _fft_stage_a(u4, ma, tw_cos, tw_sin, mch, w_f, layer):
    b, r2, r1, width = u4.shape
    sub = DFT_SUB
    groups = width // HEAD_DIM
    tw_spec = pl.BlockSpec((1,) + tw_cos.shape[1:], lambda t, bi: (t, 0, 0, 0))
    return pl.pallas_call(
        _fft_a_kernel,
        grid=(r1 // sub, b),
        in_specs=[
            pl.BlockSpec((1, r2, sub, width), lambda t, bi: (bi, 0, t, 0)),
            pl.BlockSpec(ma.shape, lambda t, bi: (0, 0, 0)),
            tw_spec, tw_spec,
            pl.BlockSpec(mch.shape, lambda t, bi: (0, 0)),
            pl.BlockSpec((None, groups, HEAD_DIM, HEAD_DIM), lambda t, bi: (layer, 0, 0, 0)),
        ],
        out_specs=pl.BlockSpec((1, 2, sub, r2, width), lambda t, bi: (bi, 0, t, 0, 0)),
        out_shape=jax.ShapeDtypeStruct((b, 2, r1, r2, width), BF16),
        compiler_params=_cparams(("arbitrary", "arbitrary")),
        name="fourier_stage_a",
    )(u4, ma, tw_cos, tw_sin, mch, w_f)


def _fft_c_kernel(t_ref, mc_ref, g_ref, o_ref):
    sub = DFT_SUB
    r1, width = t_ref.shape[2], t_ref.shape[4]
    t32 = [t_ref[0, c].astype(F32) for c in range(2)]
    for j in range(t_ref.shape[3] // sub):
        ks = slice(j * sub, (j + 1) * sub)
        tb = jnp.concatenate([t[:, ks, :].reshape(r1 * sub, width) for t in t32],
                             axis=0).astype(BF16)
        res = jnp.dot(mc_ref[...], tb, preferred_element_type=F32)
        for g in range(width // HEAD_DIM):
            sl = slice(g * HEAD_DIM, (g + 1) * HEAD_DIM)
            z = _rms(res[:, sl]) * g_ref[:, sl]
            o_ref[0, :, ks, sl] = z.reshape(r1, sub, HEAD_DIM)


def _fft_stage_c(t5, mc, gain, k2_tiles=2):
    b, _, r1, r2, width = t5.shape
    sub = DFT_SUB * k2_tiles
    return pl.pallas_call(
        _fft_c_kernel,
        grid=(r2 // sub, b),
        in_specs=[
            pl.BlockSpec((1, 2, r1, sub, width), lambda t, bi: (bi, 0, 0, t, 0)),
            pl.BlockSpec(mc.shape, lambda t, bi: (0, 0)),
            pl.BlockSpec((1, width), lambda t, bi: (0, 0)),
        ],
        out_specs=pl.BlockSpec((1, r1, sub, width), lambda t, bi: (bi, 0, t, 0)),
        out_shape=jax.ShapeDtypeStruct((b, r1, r2, width), F32),
        compiler_params=_cparams(("arbitrary", "arbitrary")),
        name="fourier_stage_c",
    )(t5, mc, gain)


def _outproj_kernel(x_ref, ya_ref, yf_ref, ym_ref, w_ref, o_ref):
    na = ya_ref.shape[1]
    nf = yf_ref.shape[1]
    acc = x_ref[...]
    acc = acc + jnp.dot(ya_ref[...], w_ref[:na, :], preferred_element_type=F32)
    acc = acc + jnp.dot(yf_ref[...].astype(BF16), w_ref[na:na + nf, :], preferred_element_type=F32)
    acc = acc + jnp.dot(ym_ref[...], w_ref[na + nf:, :], preferred_element_type=F32)
    o_ref[...] = acc


def _out_projection(x2, y_na, y_f, y_m, w_out, layer, tm=512):
    n, d = x2.shape
    row = lambda i: (i, 0)
    return pl.pallas_call(
        _outproj_kernel,
        grid=(n // tm,),
        in_specs=[
            pl.BlockSpec((tm, d), row),
            pl.BlockSpec((tm, y_na.shape[1]), row),
            pl.BlockSpec((tm, y_f.shape[1]), row),
            pl.BlockSpec((tm, y_m.shape[1]), row),
            pl.BlockSpec((None,) + w_out.shape[1:], lambda i: (layer, 0, 0)),
        ],
        out_specs=pl.BlockSpec((tm, d), row),
        out_shape=jax.ShapeDtypeStruct((n, d), F32),
        compiler_params=_cparams(("parallel",)),
        name="out_projection",
    )(x2, y_na, y_f, y_m, w_out)


def _ffn_kernel(x_ref, g_ref, wg_ref, wu_ref, wd_ref, o_ref, h_scr):
    f = pl.program_id(1)

    def delta():
        h = h_scr[...]
        gate = jnp.dot(h, wg_ref[...], preferred_element_type=F32)
        up = jnp.dot(h, wu_ref[...], preferred_element_type=F32)
        act = (gate * jax.nn.sigmoid(gate) * up).astype(BF16)
        return jnp.dot(act, wd_ref[...], preferred_element_type=F32)

    @pl.when(f == 0)
    def _():
        x = x_ref[...]
        h_scr[...] = (_rms(x) * g_ref[...]).astype(BF16)
        o_ref[...] = x_ref[...] + delta()

    @pl.when(f != 0)
    def _():
        o_ref[...] += delta()


def _ffn(x2, gain, w_gate, w_up, w_down, layer, tm=1024, tf=512):
    n, d = x2.shape
    d_ff = w_gate.shape[2]
    row = lambda i, f: (i, 0)
    return pl.pallas_call(
        _ffn_kernel,
        grid=(n // tm, d_ff // tf),
        in_specs=[
            pl.BlockSpec((tm, d), row),
            pl.BlockSpec((1, d), lambda i, f: (0, 0)),
            pl.BlockSpec((None, d, tf), lambda i, f: (layer, 0, f)),
            pl.BlockSpec((None, d, tf), lambda i, f: (layer, 0, f)),
            pl.BlockSpec((None, tf, d), lambda i, f: (layer, f, 0)),
        ],
        out_specs=pl.BlockSpec((tm, d), row),
        out_shape=jax.ShapeDtypeStruct((n, d), F32),
        scratch_shapes=[pltpu.VMEM((tm, d), BF16)],
        compiler_params=_cparams(("parallel", "arbitrary"), vmem=56 * MIB),
        name="swiglu_ffn",
    )(x2, gain, w_gate, w_up, w_down)


def kernel(x, mem, attn_norm, w_in, na_q_norm, na_k_norm, na_rpb, w_fourier, mem_norm, w_mem_kv,
           mem_q_norm, mem_k_norm, out_norm, w_out, ffn_norm, w_gate, w_up, w_down):
    b, s, d = x.shape
    m = mem.shape[1]
    depth = w_in.shape[0]
    n_mem_heads = w_mem_kv.shape[2] // 2 // HEAD_DIM
    fourier_width = w_fourier.shape[1] * HEAD_DIM
    mem_width = n_mem_heads * HEAD_DIM
    na_width = (w_in.shape[2] - fourier_width - mem_width) // 3
    na_heads = na_width // HEAD_DIM
    q_scale = HEAD_DIM ** -0.5 * LOG2E

    ma, tw_cos, tw_sin, mc, mch = _dft_matrices(s)
    x2 = x.reshape(b * s, d)
    mem2 = mem.reshape(b * m, d)
    w_in_b, w_kv_b, w_f_b, w_out_b = (w.astype(BF16) for w in (w_in, w_mem_kv, w_fourier, w_out))
    w_gate_b, w_up_b, w_down_b = (w.astype(BF16) for w in (w_gate, w_up, w_down))
    first_mem_head = (3 * na_width + fourier_width) // HEAD_DIM
    normed_heads = frozenset(range(2 * na_heads)) | frozenset(
        range(first_mem_head, first_mem_head + n_mem_heads))

    for l in range(depth):
        head_gain = jnp.concatenate([
            jnp.tile(na_q_norm[l] * q_scale, na_heads),
            jnp.tile(na_k_norm[l], na_heads),
            jnp.ones((na_width + fourier_width,), F32),
            jnp.tile(mem_q_norm[l] * q_scale, n_mem_heads),
        ]).reshape(1, -1)
        proj, u_f = _in_projection(x2, attn_norm[l].reshape(1, d), w_in_b, head_gain, normed_heads, l,
                                   u_col=3 * na_width, u_width=fourier_width)
        proj = proj.reshape(b, s, -1)

        gain_out = out_norm[l].reshape(1, -1)
        y_na = _neighbourhood_attention(proj, _na_bias_blocks(na_rpb[l]), gain_out[:, :na_width], na_heads)

        k_m, v_m = _memory_kv(mem2, mem_norm[l].reshape(1, d), w_kv_b,
                              jnp.tile(mem_k_norm[l], n_mem_heads).reshape(1, -1), l)
        y_m = _memory_attention(proj, 3 * na_width + fourier_width, k_m.reshape(b, m, mem_width),
                                v_m.reshape(b, m, mem_width), gain_out[:, na_width + fourier_width:])

        t5 = _fft_stage_a(u_f.reshape(b, s // DFT_R1, DFT_R1, fourier_width), ma, tw_cos, tw_sin,
                          mch, w_f_b, l)
        y_f = _fft_stage_c(t5, mc, gain_out[:, na_width:na_width + fourier_width])

        x2 = _out_projection(x2, y_na.reshape(b * s, na_width), y_f.reshape(b * s, fourier_width),
                             y_m.reshape(b * s, mem_width), w_out_b, l)
        x2 = _ffn(x2, ffn_norm[l].reshape(1, d), w_gate_b, w_up_b, w_down_b, l)
    return x2.reshape(b, s, d)
```

```python
import functools

import numpy as np
import jax
import jax.numpy as jnp
from jax import lax
from jax.experimental import pallas as pl
from jax.experimental.pallas import tpu as pltpu

F32 = jnp.float32
BF16 = jnp.bfloat16

EPS = 1e-6
HEAD_DIM = 128
MXU_N = 256
GRID_W = 64
NA_KH = 8
NA_KW = 16
NA_ROWS_PER_BLOCK = 8
NA_HALO_ROWS = 4
NA_HEADS_PER_STEP = 4
MASK_VALUE = -1e30
NA_INVALID_ROW = 2 * NA_KH - 1
LOG2E = 1.4426950408889634

DFT_R1 = 64
DFT_SUB = 8
DFT_A_CHUNKS = 2

MIB = 1024 * 1024
VMEM_LIMIT = 48 * MIB


def _cparams(sem, vmem=VMEM_LIMIT):
    return pltpu.CompilerParams(dimension_semantics=sem, vmem_limit_bytes=vmem)


def _rms(x, eps=EPS):
    return x * lax.rsqrt(jnp.mean(x * x, axis=-1, keepdims=True) + eps)


def _head_norm_store(dst_ref, acc, gain, n_heads, col0=0):
    for h in range(n_heads):
        sl = slice(col0 + h * HEAD_DIM, col0 + (h + 1) * HEAD_DIM)
        dl = slice(h * HEAD_DIM, (h + 1) * HEAD_DIM)
        dst_ref[:, dl] = (_rms(acc[:, sl]) * gain[:, sl]).astype(dst_ref.dtype)


def _inproj_kernel(x_ref, g_ref, w_ref, hg_ref, p_ref, u_ref, *, normed_heads, u_col, row_chunks):
    rows = x_ref.shape[0] // row_chunks
    u_width = u_ref.shape[1]
    heads_per_chunk = MXU_N // HEAD_DIM
    chunks = sorted(range(w_ref.shape[1] // MXU_N),
                    key=lambda c: not any(c * heads_per_chunk + hh in normed_heads
                                          for hh in range(heads_per_chunk)))
    for r in range(row_chunks):
        rs = slice(r * rows, (r + 1) * rows)
        h = (_rms(x_ref[rs, :]) * g_ref[...]).astype(BF16)
        for c in chunks:
            acc = jnp.dot(h, w_ref[:, c * MXU_N:(c + 1) * MXU_N], preferred_element_type=F32)
            for hh in range(heads_per_chunk):
                blk = acc[:, hh * HEAD_DIM:(hh + 1) * HEAD_DIM]
                head = c * heads_per_chunk + hh
                sl = slice(head * HEAD_DIM, (head + 1) * HEAD_DIM)
                if head in normed_heads:
                    p_ref[rs, sl] = (_rms(blk) * hg_ref[:, sl]).astype(p_ref.dtype)
                else:
                    p_ref[rs, sl] = blk.astype(p_ref.dtype)
                if u_col <= sl.start < u_col + u_width:
                    u_ref[rs, sl.start - u_col:sl.stop - u_col] = blk


def _in_projection(x2, gain, w_in, head_gain, normed_heads, layer, u_col, u_width, tm=512, row_chunks=2):
    n, d = x2.shape
    d_in = w_in.shape[2]
    return pl.pallas_call(
        functools.partial(_inproj_kernel, normed_heads=normed_heads, u_col=u_col, row_chunks=row_chunks),
        grid=(n // tm,),
        in_specs=[
            pl.BlockSpec((tm, d), lambda i: (i, 0)),
            pl.BlockSpec((1, d), lambda i: (0, 0)),
            pl.BlockSpec((None, d, d_in), lambda i: (layer, 0, 0)),
            pl.BlockSpec((1, d_in), lambda i: (0, 0)),
        ],
        out_specs=[
            pl.BlockSpec((tm, d_in), lambda i: (i, 0)),
            pl.BlockSpec((tm, u_width), lambda i: (i, 0)),
        ],
        out_shape=[
            jax.ShapeDtypeStruct((n, d_in), BF16),
            jax.ShapeDtypeStruct((n, u_width), F32),
        ],
        compiler_params=_cparams(("parallel",)),
        name="in_projection",
    )(x2, gain, w_in, head_gain)


def _na_build_bias(variant, e_ref, bias_scr, hh):
    rq, halo = NA_ROWS_PER_BLOCK, NA_HALO_ROWS
    half_rows = rq // 2
    n_pairs = (half_rows + NA_KH) // 2
    lane = lax.broadcasted_iota(jnp.int32, (GRID_W, 2 * GRID_W), 1)
    for il in range(rq):
        jl0 = jnp.where(variant == 0, max(il, halo), jnp.where(variant == 2, min(il, halo), il))
        half = il // half_rows
        rows = slice((il % half_rows) * GRID_W, (il % half_rows + 1) * GRID_W)
        for pair in range(n_pairs):
            blocks = []
            for jl in (half * halo + 2 * pair, half * halo + 2 * pair + 1):
                inside = jnp.logical_and(jl >= jl0, jl < jl0 + NA_KH)
                idx = jnp.where(inside, jl - il - halo + NA_KH - 1, NA_INVALID_ROW)
                blocks.append(e_ref[hh, idx])
            bias_scr[hh, half, rows, pair * 2 * GRID_W:(pair + 1) * 2 * GRID_W] = (
                jnp.where(lane < GRID_W, blocks[0], blocks[1]))


def _na_kernel(q_ref, kp_ref, kc_ref, kn_ref, vp_ref, vc_ref, vn_ref, e_ref, g_ref, o_ref,
               bias_scr, vx_scr):
    n_batch = q_ref.shape[0]
    n_heads = q_ref.shape[2] // HEAD_DIM
    rb = pl.program_id(1)
    last = pl.num_programs(1) - 1
    half_tokens = q_ref.shape[1] // 2
    halo_tokens = kp_ref.shape[1]
    win_tokens = half_tokens + 2 * halo_tokens
    contract_last = (((1,), (1,)), ((), ()))

    @pl.when(jnp.logical_or(rb <= 1, rb == last))
    def _():
        variant = jnp.where(rb == 0, 0, jnp.where(rb == last, 2, 1))
        for hh in range(n_heads):
            _na_build_bias(variant, e_ref, bias_scr, hh)
        vx_scr[:, :, :, HEAD_DIM:] = jnp.ones(vx_scr.shape[:3] + (HEAD_DIM,), BF16)

    k_segments = ((kp_ref, kc_ref), (kc_ref, kn_ref))
    for hh in range(n_heads):
        hs = slice(hh * HEAD_DIM, (hh + 1) * HEAD_DIM)
        for b in range(n_batch):
            row0 = 0
            for v_ref in (vp_ref, vc_ref, vn_ref):
                vx_scr[b, hh, row0:row0 + v_ref.shape[1], :HEAD_DIM] = v_ref[b, :, hs]
                row0 += v_ref.shape[1]
            for half in range(2):
                rows = slice(half * half_tokens, (half + 1) * half_tokens)
                keys = slice(half * halo_tokens, half * halo_tokens + win_tokens)
                q = q_ref[b, rows, hs]
                s = jnp.concatenate(
                    [lax.dot_general(q, k_ref[b, :, hs], contract_last, preferred_element_type=F32)
                     for k_ref in k_segments[half]], axis=-1)
                s = s + bias_scr[hh, half]
                m = jnp.max(s, axis=-1, keepdims=True)
                pb = jnp.exp2((s - m).astype(BF16))
                ol = jnp.dot(pb, vx_scr[b, hh, keys, :], preferred_element_type=F32)
                o = ol[:, :HEAD_DIM] * (1.0 / ol[:, HEAD_DIM:])
                o_ref[b, rows, hs] = (_rms(o) * g_ref[:, hs]).astype(o_ref.dtype)


def _na_bias_blocks(rpb):
    n_heads = rpb.shape[0]
    kw = min(NA_KW, GRID_W)
    col = np.arange(GRID_W)
    col_start = np.clip(col - kw // 2, 0, GRID_W - kw)
    col_ok = (col[None, :] >= col_start[:, None]) & (col[None, :] < col_start[:, None] + kw)
    col_idx = np.clip(col[None, :] - col[:, None] + NA_KW - 1, 0, 2 * NA_KW - 2)
    onehot = jnp.asarray(col_idx[None] == np.arange(2 * NA_KW - 1)[:, None, None], F32)
    picked = jnp.einsum("hde,eqk->hdqk", rpb, onehot, precision=lax.Precision.HIGHEST)
    toe = jnp.where(col_ok[None, None], picked * LOG2E, MASK_VALUE)
    toe = jnp.concatenate([toe, jnp.full((n_heads, 1, GRID_W, GRID_W), MASK_VALUE, F32)], axis=1)
    return jnp.concatenate([toe, toe], axis=-1)


def _neighbourhood_attention(proj, bias_blocks, gain, n_heads):
    b, s, _ = proj.shape
    width = n_heads * HEAD_DIM
    tq = NA_ROWS_PER_BLOCK * GRID_W
    th = NA_HALO_ROWS * GRID_W
    n_blocks = s // tq
    n_halo_blocks = s // th
    per = tq // th
    assert n_blocks >= 3 and NA_KH == 2 * NA_HALO_ROWS == NA_ROWS_PER_BLOCK

    hps = NA_HEADS_PER_STEP
    wb = hps * HEAD_DIM
    assert n_heads % hps == 0

    def cur(col0):
        return pl.BlockSpec((b, tq, wb), lambda h, rb: (0, rb, col0 + h))

    def prev(col0):
        return pl.BlockSpec((b, th, wb), lambda h, rb: (0, jnp.maximum(per * rb - 1, 0), col0 + h))

    def nxt(col0):
        return pl.BlockSpec((b, th, wb),
                            lambda h, rb: (0, jnp.minimum(per * rb + per, n_halo_blocks - 1), col0 + h))

    k0, v0 = n_heads // hps, 2 * n_heads // hps
    return pl.pallas_call(
        _na_kernel,
        grid=(n_heads // hps, n_blocks),
        in_specs=[
            cur(0), prev(k0), cur(k0), nxt(k0), prev(v0), cur(v0), nxt(v0),
            pl.BlockSpec((hps,) + bias_blocks.shape[1:], lambda h, rb: (h, 0, 0, 0)),
            pl.BlockSpec((1, wb), lambda h, rb: (0, h)),
        ],
        out_specs=cur(0),
        out_shape=jax.ShapeDtypeStruct((b, s, width), BF16),
        scratch_shapes=[pltpu.VMEM((hps, 2, tq // 2, tq // 2 + 2 * th), F32),
                        pltpu.VMEM((b, hps, tq + 2 * th, 2 * HEAD_DIM), BF16)],
        compiler_params=_cparams(("arbitrary", "arbitrary"), vmem=52 * MIB),
        name="neighbourhood_attention",
    )(proj, proj, proj, proj, proj, proj, proj, bias_blocks, gain)


def _mem_kv_kernel(m_ref, g_ref, w_ref, kg_ref, k_ref, v_ref, h_scr):
    j = pl.program_id(0)

    @pl.when(j == 0)
    def _():
        h_scr[...] = (_rms(m_ref[...]) * g_ref[...]).astype(BF16)

    acc = jnp.dot(h_scr[...], w_ref[...], preferred_element_type=F32)

    @pl.when(j == 0)
    def _():
        _head_norm_store(k_ref, acc, kg_ref[...], acc.shape[1] // HEAD_DIM)

    @pl.when(j == 1)
    def _():
        v_ref[...] = acc.astype(v_ref.dtype)


def _memory_kv(mem2, gain, w_kv, k_gain, layer):
    n, d = mem2.shape
    width = w_kv.shape[2] // 2
    full = lambda j: (0, 0)
    return pl.pallas_call(
        _mem_kv_kernel,
        grid=(2,),
        in_specs=[
            pl.BlockSpec((n, d), full),
            pl.BlockSpec((1, d), full),
            pl.BlockSpec((None, d, width), lambda j: (layer, 0, j)),
            pl.BlockSpec((1, width), full),
        ],
        out_specs=[pl.BlockSpec((n, width), full), pl.BlockSpec((n, width), full)],
        out_shape=[jax.ShapeDtypeStruct((n, width), BF16), jax.ShapeDtypeStruct((n, width), BF16)],
        scratch_shapes=[pltpu.VMEM((n, d), BF16)],
        compiler_params=_cparams(("arbitrary",)),
        name="memory_kv",
    )(mem2, gain, w_kv, k_gain)


def _mem_attn_kernel(q_ref, k_ref, v_ref, g_ref, o_ref, vx_scr):
    n_heads = q_ref.shape[2] // HEAD_DIM
    n_mem = v_ref.shape[1]
    for h in range(n_heads):
        vx_scr[h, :, :HEAD_DIM] = v_ref[0, :, h * HEAD_DIM:(h + 1) * HEAD_DIM]
        vx_scr[h, :, HEAD_DIM:] = jnp.ones((n_mem, HEAD_DIM), BF16)
    for h in range(n_heads):
        sl = slice(h * HEAD_DIM, (h + 1) * HEAD_DIM)
        s = lax.dot_general(q_ref[0, :, sl], k_ref[0, :, sl], (((1,), (1,)), ((), ())),
                            preferred_element_type=F32)
        m = jnp.max(s, axis=-1, keepdims=True)
        pb = jnp.exp2((s - m).astype(BF16))
        ol = jnp.dot(pb, vx_scr[h], preferred_element_type=F32)
        o = ol[:, :HEAD_DIM] * (1.0 / ol[:, HEAD_DIM:])
        o_ref[0, :, sl] = (_rms(o) * g_ref[:, sl]).astype(o_ref.dtype)


def _memory_attention(proj, q_col, k, v, gain, tm=2048):
    b, s, _ = proj.shape
    m, width = k.shape[1], k.shape[2]
    assert q_col % width == 0
    return pl.pallas_call(
        _mem_attn_kernel,
        grid=(b, s // tm),
        in_specs=[
            pl.BlockSpec((1, tm, width), lambda bi, t: (bi, t, q_col // width)),
            pl.BlockSpec((1, m, width), lambda bi, t: (bi, 0, 0)),
            pl.BlockSpec((1, m, width), lambda bi, t: (bi, 0, 0)),
            pl.BlockSpec((1, width), lambda bi, t: (0, 0)),
        ],
        out_specs=pl.BlockSpec((1, tm, width), lambda bi, t: (bi, t, 0)),
        out_shape=jax.ShapeDtypeStruct((b, s, width), BF16),
        scratch_shapes=[pltpu.VMEM((width // HEAD_DIM, m, 2 * HEAD_DIM), BF16)],
        compiler_params=_cparams(("parallel", "arbitrary")),
        name="memory_attention",
    )(proj, k, v, gain)


def _dft_matrices(seq):
    r1 = DFT_R1
    r2 = seq // r1
    sub = DFT_SUB
    k2 = np.arange(r2, dtype=np.int64)
    s2 = np.arange(r2, dtype=np.int64)
    s1 = np.arange(r1, dtype=np.int64)
    k1 = np.arange(r1, dtype=np.int64)
    n_ch = DFT_A_CHUNKS
    th = 2.0 * np.pi * ((k2[:, None] * s2[None, :]) % r2).astype(np.float64) / r2
    fa = np.stack([np.cos(th), -np.sin(th)], axis=0)
    ma = np.einsum("cks,ab->ckasb", fa, np.eye(sub))
    ma = ma.reshape(2, n_ch, r2 // n_ch, sub, r2 * sub).transpose(1, 0, 3, 2, 4)
    ma = ma.reshape(n_ch, 2 * r2 * sub // n_ch, r2 * sub)
    tw = 2.0 * np.pi * ((s1[:, None] * k2[None, :]) % seq).astype(np.float64) / seq
    tw = tw.reshape(r1 // sub, sub, n_ch, r2 // n_ch).transpose(0, 2, 1, 3)
    tw = tw.reshape(r1 // sub, n_ch, r2 * sub // n_ch, 1)
    tw_shape = tw.shape[:3] + (HEAD_DIM,)
    tw_cos = jnp.broadcast_to(jnp.asarray(np.cos(tw), F32), tw_shape)
    tw_sin = jnp.broadcast_to(jnp.asarray(np.sin(tw), F32), tw_shape)
    ph = 2.0 * np.pi * ((k1[:, None] * s1[None, :]) % r1).astype(np.float64) / r1
    g = np.stack([np.cos(ph), np.sin(ph)], 0)
    mc = np.einsum("cks,ab->kacsb", g, np.eye(sub))
    mc = mc.reshape(r1 * sub, 2 * sub * r1)
    cc = np.arange(HEAD_DIM, dtype=np.int64)
    pc = 2.0 * np.pi * ((cc[:, None] * cc[None, :]) % HEAD_DIM).astype(np.float64) / HEAD_DIM
    mch = np.concatenate([np.cos(pc), -np.sin(pc)], axis=0) / np.sqrt(float(seq) * HEAD_DIM)
    ma, mc, mch = (jnp.asarray(m, F32).astype(BF16) for m in (ma, mc, mch))
    return ma, tw_cos, tw_sin, mc, mch


def _fft_a_kernel(u_ref, ma_ref, twc_ref, tws_ref, mch_ref, wf_ref, t_ref):
    r2, sub, width = u_ref.shape[1], u_ref.shape[2], u_ref.shape[3]
    xb = u_ref[0].reshape(r2 * sub, width).astype(BF16)
    n_chunks = ma_ref.shape[0]
    k2c = r2 // n_chunks
    half = k2c * sub
    blocks = []
    for g in range(width // HEAD_DIM):
        gg = jnp.dot(mch_ref[...], wf_ref[g], preferred_element_type=F32)
        g_re, g_im = gg[:HEAD_DIM], gg[HEAD_DIM:]
        blocks.append(jnp.concatenate([jnp.concatenate([g_re, g_im], axis=1),
                                       jnp.concatenate([-g_im, g_re], axis=1)], axis=0).astype(BF16))
    for ch in range(n_chunks):
        res = jnp.dot(ma_ref[ch], xb, preferred_element_type=F32)
        tw_cos, tw_sin = twc_ref[0, ch], tws_ref[0, ch]
        for g in range(width // HEAD_DIM):
            sl = slice(g * HEAD_DIM, (g + 1) * HEAD_DIM)
            re, im = res[:half, sl], res[half:, sl]
            t_re = re * tw_cos + im * tw_sin
            t_im = im * tw_cos - re * tw_sin
            t2 = jnp.dot(jnp.concatenate([t_re, t_im], axis=1).astype(BF16), blocks[g],
                         preferred_element_type=F32)
            ks = slice(ch * k2c, (ch + 1) * k2c)
            for c in range(2):
                part = t2[:, c * HEAD_DIM:(c + 1) * HEAD_DIM]
                t_ref[0, c, :, ks, sl] = part.reshape(sub, k2c, HEAD_DIM).astype(t_ref.dtype)


def _fft_stage_a(u4, ma, tw_cos, tw_sin, mch, w_f, layer):
    b, r2, r1, width = u4.shape
    sub = DFT_SUB
    groups = width // HEAD_DIM
    tw_spec = pl.BlockSpec((1,) + tw_cos.shape[1:], lambda t, bi: (t, 0, 0, 0))
    return pl.pallas_call(
        _fft_a_kernel,
        grid=(r1 // sub, b),
        in_specs=[
            pl.BlockSpec((1, r2, sub, width), lambda t, bi: (bi, 0, t, 0)),
            pl.BlockSpec(ma.shape, lambda t, bi: (0, 0, 0)),
            tw_spec, tw_spec,
            pl.BlockSpec(mch.shape, lambda t, bi: (0, 0)),
            pl.BlockSpec((None, groups, HEAD_DIM, HEAD_DIM), lambda t, bi: (layer, 0, 0, 0)),
        ],
        out_specs=pl.BlockSpec((1, 2, sub, r2, width), lambda t, bi: (bi, 0, t, 0, 0)),
        out_shape=jax.ShapeDtypeStruct((b, 2, r1, r2, width), BF16),
        compiler_params=_cparams(("arbitrary", "arbitrary")),
        name="fourier_stage_a",
    )(u4, ma, tw_cos, tw_sin, mch, w_f)


def _fft_c_kernel(t_ref, mc_ref, g_ref, o_ref):
    sub = DFT_SUB
    r1, width = t_ref.shape[2], t_ref.shape[4]
    t32 = [t_ref[0, c].astype(F32) for c in range(2)]
    for j in range(t_ref.shape[3] // sub):
        ks = slice(j * sub, (j + 1) * sub)
        tb = jnp.concatenate([t[:, ks, :].reshape(r1 * sub, width) for t in t32],
                             axis=0).astype(BF16)
        res = jnp.dot(mc_ref[...], tb, preferred_element_type=F32)
        for g in range(width // HEAD_DIM):
            sl = slice(g * HEAD_DIM, (g + 1) * HEAD_DIM)
            z = _rms(res[:, sl]) * g_ref[:, sl]
            o_ref[0, :, ks, sl] = z.reshape(r1, sub, HEAD_DIM)


def _fft_stage_c(t5, mc, gain, k2_tiles=4):
    b, _, r1, r2, width = t5.shape
    sub = DFT_SUB * k2_tiles
    return pl.pallas_call(
        _fft_c_kernel,
        grid=(r2 // sub, b),
        in_specs=[
            pl.BlockSpec((1, 2, r1, sub, width), lambda t, bi: (bi, 0, 0, t, 0)),
            pl.BlockSpec(mc.shape, lambda t, bi: (0, 0)),
            pl.BlockSpec((1, width), lambda t, bi: (0, 0)),
        ],
        out_specs=pl.BlockSpec((1, r1, sub, width), lambda t, bi: (bi, 0, t, 0)),
        out_shape=jax.ShapeDtypeStruct((b, r1, r2, width), F32),
        compiler_params=_cparams(("arbitrary", "arbitrary")),
        name="fourier_stage_c",
    )(t5, mc, gain)


def _outproj_kernel(x_ref, ya_ref, yf_ref, ym_ref, w_ref, o_ref):
    na = ya_ref.shape[1]
    nf = yf_ref.shape[1]
    acc = x_ref[...]
    acc = acc + jnp.dot(ya_ref[...], w_ref[:na, :], preferred_element_type=F32)
    acc = acc + jnp.dot(yf_ref[...].astype(BF16), w_ref[na:na + nf, :], preferred_element_type=F32)
    acc = acc + jnp.dot(ym_ref[...], w_ref[na + nf:, :], preferred_element_type=F32)
    o_ref[...] = acc


def _out_projection(x2, y_na, y_f, y_m, w_out, layer, tm=512):
    n, d = x2.shape
    row = lambda i: (i, 0)
    return pl.pallas_call(
        _outproj_kernel,
        grid=(n // tm,),
        in_specs=[
            pl.BlockSpec((tm, d), row),
            pl.BlockSpec((tm, y_na.shape[1]), row),
            pl.BlockSpec((tm, y_f.shape[1]), row),
            pl.BlockSpec((tm, y_m.shape[1]), row),
            pl.BlockSpec((None,) + w_out.shape[1:], lambda i: (layer, 0, 0)),
        ],
        out_specs=pl.BlockSpec((tm, d), row),
        out_shape=jax.ShapeDtypeStruct((n, d), F32),
        compiler_params=_cparams(("parallel",)),
        name="out_projection",
    )(x2, y_na, y_f, y_m, w_out)


def _ffn_kernel(x_ref, g_ref, wg_ref, wu_ref, wd_ref, o_ref, h_scr):
    f = pl.program_id(1)

    def delta():
        h = h_scr[...]
        gate = jnp.dot(h, wg_ref[...], preferred_element_type=F32)
        up = jnp.dot(h, wu_ref[...], preferred_element_type=F32)
        act = (gate * jax.nn.sigmoid(gate) * up).astype(BF16)
        return jnp.dot(act, wd_ref[...], preferred_element_type=F32)

    @pl.when(f == 0)
    def _():
        x = x_ref[...]
        h_scr[...] = (_rms(x) * g_ref[...]).astype(BF16)
        o_ref[...] = x_ref[...] + delta()

    @pl.when(f != 0)
    def _():
        o_ref[...] += delta()


def _ffn(x2, gain, w_gate, w_up, w_down, layer, tm=1024, tf=512):
    n, d = x2.shape
    d_ff = w_gate.shape[2]
    row = lambda i, f: (i, 0)
    return pl.pallas_call(
        _ffn_kernel,
        grid=(n // tm, d_ff // tf),
        in_specs=[
            pl.BlockSpec((tm, d), row),
            pl.BlockSpec((1, d), lambda i, f: (0, 0)),
            pl.BlockSpec((None, d, tf), lambda i, f: (layer, 0, f)),
            pl.BlockSpec((None, d, tf), lambda i, f: (layer, 0, f)),
            pl.BlockSpec((None, tf, d), lambda i, f: (layer, f, 0)),
        ],
        out_specs=pl.BlockSpec((tm, d), row),
        out_shape=jax.ShapeDtypeStruct((n, d), F32),
        scratch_shapes=[pltpu.VMEM((tm, d), BF16)],
        compiler_params=_cparams(("parallel", "arbitrary"), vmem=56 * MIB),
        name="swiglu_ffn",
    )(x2, gain, w_gate, w_up, w_down)


def kernel(x, mem, attn_norm, w_in, na_q_norm, na_k_norm, na_rpb, w_fourier, mem_norm, w_mem_kv,
           mem_q_norm, mem_k_norm, out_norm, w_out, ffn_norm, w_gate, w_up, w_down):
    b, s, d = x.shape
    m = mem.shape[1]
    depth = w_in.shape[0]
    n_mem_heads = w_mem_kv.shape[2] // 2 // HEAD_DIM
    fourier_width = w_fourier.shape[1] * HEAD_DIM
    mem_width = n_mem_heads * HEAD_DIM
    na_width = (w_in.shape[2] - fourier_width - mem_width) // 3
    na_heads = na_width // HEAD_DIM
    q_scale = HEAD_DIM ** -0.5 * LOG2E

    ma, tw_cos, tw_sin, mc, mch = _dft_matrices(s)
    x2 = x.reshape(b * s, d)
    mem2 = mem.reshape(b * m, d)
    w_in_b, w_kv_b, w_f_b, w_out_b = (w.astype(BF16) for w in (w_in, w_mem_kv, w_fourier, w_out))
    w_gate_b, w_up_b, w_down_b = (w.astype(BF16) for w in (w_gate, w_up, w_down))
    first_mem_head = (3 * na_width + fourier_width) // HEAD_DIM
    normed_heads = frozenset(range(2 * na_heads)) | frozenset(
        range(first_mem_head, first_mem_head + n_mem_heads))

    for l in range(depth):
        head_gain = jnp.concatenate([
            jnp.tile(na_q_norm[l] * q_scale, na_heads),
            jnp.tile(na_k_norm[l], na_heads),
            jnp.ones((na_width + fourier_width,), F32),
            jnp.tile(mem_q_norm[l] * q_scale, n_mem_heads),
        ]).reshape(1, -1)
        proj, u_f = _in_projection(x2, attn_norm[l].reshape(1, d), w_in_b, head_gain, normed_heads, l,
                                   u_col=3 * na_width, u_width=fourier_width)
        proj = proj.reshape(b, s, -1)

        gain_out = out_norm[l].reshape(1, -1)
        y_na = _neighbourhood_attention(proj, _na_bias_blocks(na_rpb[l]), gain_out[:, :na_width], na_heads)

        k_m, v_m = _memory_kv(mem2, mem_norm[l].reshape(1, d), w_kv_b,
                              jnp.tile(mem_k_norm[l], n_mem_heads).reshape(1, -1), l)
        y_m = _memory_attention(proj, 3 * na_width + fourier_width, k_m.reshape(b, m, mem_width),
                                v_m.reshape(b, m, mem_width), gain_out[:, na_width + fourier_width:])

        t5 = _fft_stage_a(u_f.reshape(b, s // DFT_R1, DFT_R1, fourier_width), ma, tw_cos, tw_sin,
                          mch, w_f_b, l)
        y_f = _fft_stage_c(t5, mc, gain_out[:, na_width:na_width + fourier_width])

        x2 = _out_projection(x2, y_na.reshape(b * s, na_width), y_f.reshape(b * s, fourier_width),
                             y_m.reshape(b * s, mem_width), w_out_b, l)
        x2 = _ffn(x2, ffn_norm[l].reshape(1, d), w_gate_b, w_up_b, w_down_b, l)
    return x2.reshape(b, s, d)
```

```python
import functools

import numpy as np
import jax
import jax.numpy as jnp
from jax import lax
from jax.experimental import pallas as pl
from jax.experimental.pallas import tpu as pltpu

F32 = jnp.float32
BF16 = jnp.bfloat16

EPS = 1e-6
HEAD_DIM = 128
MXU_N = 256
GRID_W = 64
NA_KH = 8
NA_KW = 16
NA_ROWS_PER_BLOCK = 8
NA_HALO_ROWS = 4
NA_HEADS_PER_STEP = 4
MASK_VALUE = -1e30
NA_INVALID_ROW = 2 * NA_KH - 1
LOG2E = 1.4426950408889634

DFT_R1 = 64
DFT_SUB = 8
DFT_A_CHUNKS = 2

MIB = 1024 * 1024
VMEM_LIMIT = 48 * MIB


def _cparams(sem, vmem=VMEM_LIMIT):
    return pltpu.CompilerParams(dimension_semantics=sem, vmem_limit_bytes=vmem)


def _rms(x, eps=EPS):
    return x * lax.rsqrt(jnp.mean(x * x, axis=-1, keepdims=True) + eps)


def _head_norm_store(dst_ref, acc, gain, n_heads, col0=0):
    for h in range(n_heads):
        sl = slice(col0 + h * HEAD_DIM, col0 + (h + 1) * HEAD_DIM)
        dl = slice(h * HEAD_DIM, (h + 1) * HEAD_DIM)
        dst_ref[:, dl] = (_rms(acc[:, sl]) * gain[:, sl]).astype(dst_ref.dtype)


def _inproj_kernel(x_ref, g_ref, w_ref, hg_ref, p_ref, u_ref, *, normed_heads, u_col, row_chunks):
    rows = x_ref.shape[0] // row_chunks
    u_width = u_ref.shape[1]
    heads_per_chunk = MXU_N // HEAD_DIM
    chunks = sorted(range(w_ref.shape[1] // MXU_N),
                    key=lambda c: not any(c * heads_per_chunk + hh in normed_heads
                                          for hh in range(heads_per_chunk)))
    for r in range(row_chunks):
        rs = slice(r * rows, (r + 1) * rows)
        h = (_rms(x_ref[rs, :]) * g_ref[...]).astype(BF16)
        for c in chunks:
            acc = jnp.dot(h, w_ref[:, c * MXU_N:(c + 1) * MXU_N], preferred_element_type=F32)
            for hh in range(heads_per_chunk):
                blk = acc[:, hh * HEAD_DIM:(hh + 1) * HEAD_DIM]
                head = c * heads_per_chunk + hh
                sl = slice(head * HEAD_DIM, (head + 1) * HEAD_DIM)
                if head in normed_heads:
                    p_ref[rs, sl] = (_rms(blk) * hg_ref[:, sl]).astype(p_ref.dtype)
                else:
                    p_ref[rs, sl] = blk.astype(p_ref.dtype)
                if u_col <= sl.start < u_col + u_width:
                    u_ref[rs, sl.start - u_col:sl.stop - u_col] = blk


def _in_projection(x2, gain, w_in, head_gain, normed_heads, layer, u_col, u_width, tm=512, row_chunks=2):
    n, d = x2.shape
    d_in = w_in.shape[2]
    return pl.pallas_call(
        functools.partial(_inproj_kernel, normed_heads=normed_heads, u_col=u_col, row_chunks=row_chunks),
        grid=(n // tm,),
        in_specs=[
            pl.BlockSpec((tm, d), lambda i: (i, 0)),
            pl.BlockSpec((1, d), lambda i: (0, 0)),
            pl.BlockSpec((None, d, d_in), lambda i: (layer, 0, 0)),
            pl.BlockSpec((1, d_in), lambda i: (0, 0)),
        ],
        out_specs=[
            pl.BlockSpec((tm, d_in), lambda i: (i, 0)),
            pl.BlockSpec((tm, u_width), lambda i: (i, 0)),
        ],
        out_shape=[
            jax.ShapeDtypeStruct((n, d_in), BF16),
            jax.ShapeDtypeStruct((n, u_width), F32),
        ],
        compiler_params=_cparams(("parallel",)),
        name="in_projection",
    )(x2, gain, w_in, head_gain)


def _na_build_bias(variant, e_ref, bias_scr, hh):
    rq, halo = NA_ROWS_PER_BLOCK, NA_HALO_ROWS
    half_rows = rq // 2
    n_pairs = (half_rows + NA_KH) // 2
    lane = lax.broadcasted_iota(jnp.int32, (GRID_W, 2 * GRID_W), 1)
    for il in range(rq):
        jl0 = jnp.where(variant == 0, max(il, halo), jnp.where(variant == 2, min(il, halo), il))
        half = il // half_rows
        rows = slice((il % half_rows) * GRID_W, (il % half_rows + 1) * GRID_W)
        for pair in range(n_pairs):
            blocks = []
            for jl in (half * halo + 2 * pair, half * halo + 2 * pair + 1):
                inside = jnp.logical_and(jl >= jl0, jl < jl0 + NA_KH)
                idx = jnp.where(inside, jl - il - halo + NA_KH - 1, NA_INVALID_ROW)
                blocks.append(e_ref[hh, idx])
            bias_scr[hh, half, rows, pair * 2 * GRID_W:(pair + 1) * 2 * GRID_W] = (
                jnp.where(lane < GRID_W, blocks[0], blocks[1]))


def _na_kernel(q_ref, kp_ref, kc_ref, kn_ref, vp_ref, vc_ref, vn_ref, e_ref, g_ref, o_ref,
               bias_scr, vx_scr):
    n_batch = q_ref.shape[0]
    n_heads = q_ref.shape[2] // HEAD_DIM
    rb = pl.program_id(1)
    last = pl.num_programs(1) - 1
    half_tokens = q_ref.shape[1] // 2
    halo_tokens = kp_ref.shape[1]
    win_tokens = half_tokens + 2 * halo_tokens
    contract_last = (((1,), (1,)), ((), ()))

    @pl.when(jnp.logical_or(rb <= 1, rb == last))
    def _():
        variant = jnp.where(rb == 0, 0, jnp.where(rb == last, 2, 1))
        for hh in range(n_heads):
            _na_build_bias(variant, e_ref, bias_scr, hh)
        vx_scr[:, :, :, HEAD_DIM:] = jnp.ones(vx_scr.shape[:3] + (HEAD_DIM,), BF16)

    k_segments = ((kp_ref, kc_ref), (kc_ref, kn_ref))
    for hh in range(n_heads):
        hs = slice(hh * HEAD_DIM, (hh + 1) * HEAD_DIM)
        for b in range(n_batch):
            row0 = 0
            for v_ref in (vp_ref, vc_ref, vn_ref):
                vx_scr[b, hh, row0:row0 + v_ref.shape[1], :HEAD_DIM] = v_ref[b, :, hs]
                row0 += v_ref.shape[1]
            for half in range(2):
                rows = slice(half * half_tokens, (half + 1) * half_tokens)
                keys = slice(half * halo_tokens, half * halo_tokens + win_tokens)
                q = q_ref[b, rows, hs]
                s = jnp.concatenate(
                    [lax.dot_general(q, k_ref[b, :, hs], contract_last, preferred_element_type=F32)
                     for k_ref in k_segments[half]], axis=-1)
                s = s + bias_scr[hh, half]
                m = jnp.max(s, axis=-1, keepdims=True)
                pb = jnp.exp2((s - m).astype(BF16))
                ol = jnp.dot(pb, vx_scr[b, hh, keys, :], preferred_element_type=F32)
                o = ol[:, :HEAD_DIM] * (1.0 / ol[:, HEAD_DIM:])
                o_ref[b, rows, hs] = (_rms(o) * g_ref[:, hs]).astype(o_ref.dtype)


def _na_bias_blocks(rpb):
    n_heads = rpb.shape[0]
    kw = min(NA_KW, GRID_W)
    col = np.arange(GRID_W)
    col_start = np.clip(col - kw // 2, 0, GRID_W - kw)
    col_ok = (col[None, :] >= col_start[:, None]) & (col[None, :] < col_start[:, None] + kw)
    col_idx = np.clip(col[None, :] - col[:, None] + NA_KW - 1, 0, 2 * NA_KW - 2)
    onehot = jnp.asarray(col_idx[None] == np.arange(2 * NA_KW - 1)[:, None, None], F32)
    picked = jnp.einsum("hde,eqk->hdqk", rpb, onehot, precision=lax.Precision.HIGHEST)
    toe = jnp.where(col_ok[None, None], picked * LOG2E, MASK_VALUE)
    toe = jnp.concatenate([toe, jnp.full((n_heads, 1, GRID_W, GRID_W), MASK_VALUE, F32)], axis=1)
    return jnp.concatenate([toe, toe], axis=-1)


def _neighbourhood_attention(proj, bias_blocks, gain, n_heads):
    b, s, _ = proj.shape
    width = n_heads * HEAD_DIM
    tq = NA_ROWS_PER_BLOCK * GRID_W
    th = NA_HALO_ROWS * GRID_W
    n_blocks = s // tq
    n_halo_blocks = s // th
    per = tq // th
    assert n_blocks >= 3 and NA_KH == 2 * NA_HALO_ROWS == NA_ROWS_PER_BLOCK

    hps = NA_HEADS_PER_STEP
    wb = hps * HEAD_DIM
    assert n_heads % hps == 0

    def cur(col0):
        return pl.BlockSpec((b, tq, wb), lambda h, rb: (0, rb, col0 + h))

    def prev(col0):
        return pl.BlockSpec((b, th, wb), lambda h, rb: (0, jnp.maximum(per * rb - 1, 0), col0 + h))

    def nxt(col0):
        return pl.BlockSpec((b, th, wb),
                            lambda h, rb: (0, jnp.minimum(per * rb + per, n_halo_blocks - 1), col0 + h))

    k0, v0 = n_heads // hps, 2 * n_heads // hps
    return pl.pallas_call(
        _na_kernel,
        grid=(n_heads // hps, n_blocks),
        in_specs=[
            cur(0), prev(k0), cur(k0), nxt(k0), prev(v0), cur(v0), nxt(v0),
            pl.BlockSpec((hps,) + bias_blocks.shape[1:], lambda h, rb: (h, 0, 0, 0)),
            pl.BlockSpec((1, wb), lambda h, rb: (0, h)),
        ],
        out_specs=cur(0),
        out_shape=jax.ShapeDtypeStruct((b, s, width), BF16),
        scratch_shapes=[pltpu.VMEM((hps, 2, tq // 2, tq // 2 + 2 * th), F32),
                        pltpu.VMEM((b, hps, tq + 2 * th, 2 * HEAD_DIM), BF16)],
        compiler_params=_cparams(("arbitrary", "arbitrary"), vmem=52 * MIB),
        name="neighbourhood_attention",
    )(proj, proj, proj, proj, proj, proj, proj, bias_blocks, gain)


def _mem_kv_kernel(m_ref, g_ref, w_ref, kg_ref, k_ref, v_ref, h_scr):
    j = pl.program_id(0)

    @pl.when(j == 0)
    def _():
        h_scr[...] = (_rms(m_ref[...]) * g_ref[...]).astype(BF16)

    acc = jnp.dot(h_scr[...], w_ref[...], preferred_element_type=F32)

    @pl.when(j == 0)
    def _():
        _head_norm_store(k_ref, acc, kg_ref[...], acc.shape[1] // HEAD_DIM)

    @pl.when(j == 1)
    def _():
        v_ref[...] = acc.astype(v_ref.dtype)


def _memory_kv(mem2, gain, w_kv, k_gain, layer):
    n, d = mem2.shape
    width = w_kv.shape[2] // 2
    full = lambda j: (0, 0)
    return pl.pallas_call(
        _mem_kv_kernel,
        grid=(2,),
        in_specs=[
            pl.BlockSpec((n, d), full),
            pl.BlockSpec((1, d), full),
            pl.BlockSpec((None, d, width), lambda j: (layer, 0, j)),
            pl.BlockSpec((1, width), full),
        ],
        out_specs=[pl.BlockSpec((n, width), full), pl.BlockSpec((n, width), full)],
        out_shape=[jax.ShapeDtypeStruct((n, width), BF16), jax.ShapeDtypeStruct((n, width), BF16)],
        scratch_shapes=[pltpu.VMEM((n, d), BF16)],
        compiler_params=_cparams(("arbitrary",)),
        name="memory_kv",
    )(mem2, gain, w_kv, k_gain)


def _mem_attn_kernel(q_ref, k_ref, v_ref, g_ref, o_ref, vx_scr):
    n_heads = q_ref.shape[2] // HEAD_DIM
    n_mem = v_ref.shape[1]
    for h in range(n_heads):
        vx_scr[h, :, :HEAD_DIM] = v_ref[0, :, h * HEAD_DIM:(h + 1) * HEAD_DIM]
        vx_scr[h, :, HEAD_DIM:] = jnp.ones((n_mem, HEAD_DIM), BF16)
    for h in range(n_heads):
        sl = slice(h * HEAD_DIM, (h + 1) * HEAD_DIM)
        s = lax.dot_general(q_ref[0, :, sl], k_ref[0, :, sl], (((1,), (1,)), ((), ())),
                            preferred_element_type=F32)
        m = jnp.max(s, axis=-1, keepdims=True)
        pb = jnp.exp2((s - m).astype(BF16))
        ol = jnp.dot(pb, vx_scr[h], preferred_element_type=F32)
        o = ol[:, :HEAD_DIM] * (1.0 / ol[:, HEAD_DIM:])
        o_ref[0, :, sl] = (_rms(o) * g_ref[:, sl]).astype(o_ref.dtype)


def _memory_attention(proj, q_col, k, v, gain, tm=2048):
    b, s, _ = proj.shape
    m, width = k.shape[1], k.shape[2]
    assert q_col % width == 0
    return pl.pallas_call(
        _mem_attn_kernel,
        grid=(b, s // tm),
        in_specs=[
            pl.BlockSpec((1, tm, width), lambda bi, t: (bi, t, q_col // width)),
            pl.BlockSpec((1, m, width), lambda bi, t: (bi, 0, 0)),
            pl.BlockSpec((1, m, width), lambda bi, t: (bi, 0, 0)),
            pl.BlockSpec((1, width), lambda bi, t: (0, 0)),
        ],
        out_specs=pl.BlockSpec((1, tm, width), lambda bi, t: (bi, t, 0)),
        out_shape=jax.ShapeDtypeStruct((b, s, width), BF16),
        scratch_shapes=[pltpu.VMEM((width // HEAD_DIM, m, 2 * HEAD_DIM), BF16)],
        compiler_params=_cparams(("parallel", "arbitrary")),
        name="memory_attention",
    )(proj, k, v, gain)


def _dft_matrices(seq):
    r1 = DFT_R1
    r2 = seq // r1
    sub = DFT_SUB
    k2 = np.arange(r2, dtype=np.int64)
    s2 = np.arange(r2, dtype=np.int64)
    s1 = np.arange(r1, dtype=np.int64)
    k1 = np.arange(r1, dtype=np.int64)
    n_ch = DFT_A_CHUNKS
    th = 2.0 * np.pi * ((k2[:, None] * s2[None, :]) % r2).astype(np.float64) / r2
    fa = np.stack([np.cos(th), -np.sin(th)], axis=0)
    ma = np.einsum("cks,ab->ckasb", fa, np.eye(sub))
    ma = ma.reshape(2, n_ch, r2 // n_ch, sub, r2 * sub).transpose(1, 0, 3, 2, 4)
    ma = ma.reshape(n_ch, 2 * r2 * sub // n_ch, r2 * sub)
    tw = 2.0 * np.pi * ((s1[:, None] * k2[None, :]) % seq).astype(np.float64) / seq
    tw = tw.reshape(r1 // sub, sub, n_ch, r2 // n_ch).transpose(0, 2, 1, 3)
    tw = tw.reshape(r1 // sub, n_ch, r2 * sub // n_ch, 1)
    tw_shape = tw.shape[:3] + (HEAD_DIM,)
    tw_cos = jnp.broadcast_to(jnp.asarray(np.cos(tw), F32), tw_shape)
    tw_sin = jnp.broadcast_to(jnp.asarray(np.sin(tw), F32), tw_shape)
    ph = 2.0 * np.pi * ((k1[:, None] * s1[None, :]) % r1).astype(np.float64) / r1
    g = np.stack([np.cos(ph), np.sin(ph)], 0)
    mc = np.einsum("cks,ab->kacsb", g, np.eye(sub))
    mc = mc.reshape(r1 * sub, 2 * sub * r1)
    cc = np.arange(HEAD_DIM, dtype=np.int64)
    pc = 2.0 * np.pi * ((cc[:, None] * cc[None, :]) % HEAD_DIM).astype(np.float64) / HEAD_DIM
    mch = np.concatenate([np.cos(pc), -np.sin(pc)], axis=0) / np.sqrt(float(seq) * HEAD_DIM)
    ma, mc, mch = (jnp.asarray(m, F32).astype(BF16) for m in (ma, mc, mch))
    return ma, tw_cos, tw_sin, mc, mch


def _fft_a_kernel(u_ref, ma_ref, twc_ref, tws_ref, mch_ref, wf_ref, t_ref):
    r2, width = u_ref.shape[1], u_ref.shape[3]
    sub = DFT_SUB
    n_chunks = ma_ref.shape[0]
    k2c = r2 // n_chunks
    half = k2c * sub
    blocks = []
    for g in range(width // HEAD_DIM):
        gg = jnp.dot(mch_ref[...], wf_ref[g], preferred_element_type=F32)
        g_re, g_im = gg[:HEAD_DIM], gg[HEAD_DIM:]
        blocks.append(jnp.concatenate([jnp.concatenate([g_re, g_im], axis=1),
                                       jnp.concatenate([-g_im, g_re], axis=1)], axis=0).astype(BF16))
    for j in range(u_ref.shape[2] // sub):
        ss = slice(j * sub, (j + 1) * sub)
        xb = u_ref[0, :, ss, :].reshape(r2 * sub, width).astype(BF16)
        for ch in range(n_chunks):
            res = jnp.dot(ma_ref[ch], xb, preferred_element_type=F32)
            tw_cos, tw_sin = twc_ref[j, ch], tws_ref[j, ch]
            for g in range(width // HEAD_DIM):
                sl = slice(g * HEAD_DIM, (g + 1) * HEAD_DIM)
                re, im = res[:half, sl], res[half:, sl]
                t_re = re * tw_cos + im * tw_sin
                t_im = im * tw_cos - re * tw_sin
                t2 = jnp.dot(jnp.concatenate([t_re, t_im], axis=1).astype(BF16), blocks[g],
                             preferred_element_type=F32)
                ks = slice(ch * k2c, (ch + 1) * k2c)
                for c in range(2):
                    part = t2[:, c * HEAD_DIM:(c + 1) * HEAD_DIM]
                    t_ref[0, c, ss, ks, sl] = part.reshape(sub, k2c, HEAD_DIM).astype(t_ref.dtype)


def _fft_stage_a(u4, ma, tw_cos, tw_sin, mch, w_f, layer, s1_tiles=2):
    b, r2, r1, width = u4.shape
    sub = DFT_SUB * s1_tiles
    groups = width // HEAD_DIM
    tw_spec = pl.BlockSpec((s1_tiles,) + tw_cos.shape[1:], lambda t, bi: (t, 0, 0, 0))
    return pl.pallas_call(
        _fft_a_kernel,
        grid=(r1 // sub, b),
        in_specs=[
            pl.BlockSpec((1, r2, sub, width), lambda t, bi: (bi, 0, t, 0)),
            pl.BlockSpec(ma.shape, lambda t, bi: (0, 0, 0)),
            tw_spec, tw_spec,
            pl.BlockSpec(mch.shape, lambda t, bi: (0, 0)),
            pl.BlockSpec((None, groups, HEAD_DIM, HEAD_DIM), lambda t, bi: (layer, 0, 0, 0)),
        ],
        out_specs=pl.BlockSpec((1, 2, sub, r2, width), lambda t, bi: (bi, 0, t, 0, 0)),
        out_shape=jax.ShapeDtypeStruct((b, 2, r1, r2, width), BF16),
        compiler_params=_cparams(("arbitrary", "arbitrary")),
        name="fourier_stage_a",
    )(u4, ma, tw_cos, tw_sin, mch, w_f)


def _fft_c_kernel(t_ref, mc_ref, g_ref, o_ref):
    sub = DFT_SUB
    r1, width = t_ref.shape[2], t_ref.shape[4]
    t32 = [t_ref[0, c].astype(F32) for c in range(2)]
    for j in range(t_ref.shape[3] // sub):
        ks = slice(j * sub, (j + 1) * sub)
        tb = jnp.concatenate([t[:, ks, :].reshape(r1 * sub, width) for t in t32],
                             axis=0).astype(BF16)
        res = jnp.dot(mc_ref[...], tb, preferred_element_type=F32)
        for g in range(width // HEAD_DIM):
            sl = slice(g * HEAD_DIM, (g + 1) * HEAD_DIM)
            z = _rms(res[:, sl]) * g_ref[:, sl]
            o_ref[0, :, ks, sl] = z.reshape(r1, sub, HEAD_DIM)


def _fft_stage_c(t5, mc, gain, k2_tiles=4):
    b, _, r1, r2, width = t5.shape
    sub = DFT_SUB * k2_tiles
    return pl.pallas_call(
        _fft_c_kernel,
        grid=(r2 // sub, b),
        in_specs=[
            pl.BlockSpec((1, 2, r1, sub, width), lambda t, bi: (bi, 0, 0, t, 0)),
            pl.BlockSpec(mc.shape, lambda t, bi: (0, 0)),
            pl.BlockSpec((1, width), lambda t, bi: (0, 0)),
        ],
        out_specs=pl.BlockSpec((1, r1, sub, width), lambda t, bi: (bi, 0, t, 0)),
        out_shape=jax.ShapeDtypeStruct((b, r1, r2, width), F32),
        compiler_params=_cparams(("arbitrary", "arbitrary")),
        name="fourier_stage_c",
    )(t5, mc, gain)


def _outproj_kernel(x_ref, ya_ref, yf_ref, ym_ref, w_ref, o_ref):
    na = ya_ref.shape[1]
    nf = yf_ref.shape[1]
    acc = x_ref[...]
    acc = acc + jnp.dot(ya_ref[...], w_ref[:na, :], preferred_element_type=F32)
    acc = acc + jnp.dot(yf_ref[...].astype(BF16), w_ref[na:na + nf, :], preferred_element_type=F32)
    acc = acc + jnp.dot(ym_ref[...], w_ref[na + nf:, :], preferred_element_type=F32)
    o_ref[...] = acc


def _out_projection(x2, y_na, y_f, y_m, w_out, layer, tm=512):
    n, d = x2.shape
    row = lambda i: (i, 0)
    return pl.pallas_call(
        _outproj_kernel,
        grid=(n // tm,),
        in_specs=[
            pl.BlockSpec((tm, d), row),
            pl.BlockSpec((tm, y_na.shape[1]), row),
            pl.BlockSpec((tm, y_f.shape[1]), row),
            pl.BlockSpec((tm, y_m.shape[1]), row),
            pl.BlockSpec((None,) + w_out.shape[1:], lambda i: (layer, 0, 0)),
        ],
        out_specs=pl.BlockSpec((tm, d), row),
        out_shape=jax.ShapeDtypeStruct((n, d), F32),
        compiler_params=_cparams(("parallel",)),
        name="out_projection",
    )(x2, y_na, y_f, y_m, w_out)


def _ffn_kernel(x_ref, g_ref, wg_ref, wu_ref, wd_ref, o_ref, h_scr):
    f = pl.program_id(1)

    def delta():
        h = h_scr[...]
        gate = jnp.dot(h, wg_ref[...], preferred_element_type=F32)
        up = jnp.dot(h, wu_ref[...], preferred_element_type=F32)
        act = (gate * jax.nn.sigmoid(gate) * up).astype(BF16)
        return jnp.dot(act, wd_ref[...], preferred_element_type=F32)

    @pl.when(f == 0)
    def _():
        x = x_ref[...]
        h_scr[...] = (_rms(x) * g_ref[...]).astype(BF16)
        o_ref[...] = x_ref[...] + delta()

    @pl.when(f != 0)
    def _():
        o_ref[...] += delta()


def _ffn(x2, gain, w_gate, w_up, w_down, layer, tm=1024, tf=512):
    n, d = x2.shape
    d_ff = w_gate.shape[2]
    row = lambda i, f: (i, 0)
    return pl.pallas_call(
        _ffn_kernel,
        grid=(n // tm, d_ff // tf),
        in_specs=[
            pl.BlockSpec((tm, d), row),
            pl.BlockSpec((1, d), lambda i, f: (0, 0)),
            pl.BlockSpec((None, d, tf), lambda i, f: (layer, 0, f)),
            pl.BlockSpec((None, d, tf), lambda i, f: (layer, 0, f)),
            pl.BlockSpec((None, tf, d), lambda i, f: (layer, f, 0)),
        ],
        out_specs=pl.BlockSpec((tm, d), row),
        out_shape=jax.ShapeDtypeStruct((n, d), F32),
        scratch_shapes=[pltpu.VMEM((tm, d), BF16)],
        compiler_params=_cparams(("parallel", "arbitrary"), vmem=56 * MIB),
        name="swiglu_ffn",
    )(x2, gain, w_gate, w_up, w_down)


def kernel(x, mem, attn_norm, w_in, na_q_norm, na_k_norm, na_rpb, w_fourier, mem_norm, w_mem_kv,
           mem_q_norm, mem_k_norm, out_norm, w_out, ffn_norm, w_gate, w_up, w_down):
    b, s, d = x.shape
    m = mem.shape[1]
    depth = w_in.shape[0]
    n_mem_heads = w_mem_kv.shape[2] // 2 // HEAD_DIM
    fourier_width = w_fourier.shape[1] * HEAD_DIM
    mem_width = n_mem_heads * HEAD_DIM
    na_width = (w_in.shape[2] - fourier_width - mem_width) // 3
    na_heads = na_width // HEAD_DIM
    q_scale = HEAD_DIM ** -0.5 * LOG2E

    ma, tw_cos, tw_sin, mc, mch = _dft_matrices(s)
    x2 = x.reshape(b * s, d)
    mem2 = mem.reshape(b * m, d)
    w_in_b, w_kv_b, w_f_b, w_out_b = (w.astype(BF16) for w in (w_in, w_mem_kv, w_fourier, w_out))
    w_gate_b, w_up_b, w_down_b = (w.astype(BF16) for w in (w_gate, w_up, w_down))
    first_mem_head = (3 * na_width + fourier_width) // HEAD_DIM
    normed_heads = frozenset(range(2 * na_heads)) | frozenset(
        range(first_mem_head, first_mem_head + n_mem_heads))

    for l in range(depth):
        head_gain = jnp.concatenate([
            jnp.tile(na_q_norm[l] * q_scale, na_heads),
            jnp.tile(na_k_norm[l], na_heads),
            jnp.ones((na_width + fourier_width,), F32),
            jnp.tile(mem_q_norm[l] * q_scale, n_mem_heads),
        ]).reshape(1, -1)
        proj, u_f = _in_projection(x2, attn_norm[l].reshape(1, d), w_in_b, head_gain, normed_heads, l,
                                   u_col=3 * na_width, u_width=fourier_width)
        proj = proj.reshape(b, s, -1)

        gain_out = out_norm[l].reshape(1, -1)
        y_na = _neighbourhood_attention(proj, _na_bias_blocks(na_rpb[l]), gain_out[:, :na_width], na_heads)

        k_m, v_m = _memory_kv(mem2, mem_norm[l].reshape(1, d), w_kv_b,
                              jnp.tile(mem_k_norm[l], n_mem_heads).reshape(1, -1), l)
        y_m = _memory_attention(proj, 3 * na_width + fourier_width, k_m.reshape(b, m, mem_width),
                                v_m.reshape(b, m, mem_width), gain_out[:, na_width + fourier_width:])

        t5 = _fft_stage_a(u_f.reshape(b, s // DFT_R1, DFT_R1, fourier_width), ma, tw_cos, tw_sin,
                          mch, w_f_b, l)
        y_f = _fft_stage_c(t5, mc, gain_out[:, na_width:na_width + fourier_width])

        x2 = _out_projection(x2, y_na.reshape(b * s, na_width), y_f.reshape(b * s, fourier_width),
                             y_m.reshape(b * s, mem_width), w_out_b, l)
        x2 = _ffn(x2, ffn_norm[l].reshape(1, d), w_gate_b, w_up_b, w_down_b, l)
    return x2.reshape(b, s, d)
```

```python
import functools

import numpy as np
import jax
import jax.numpy as jnp
from jax import lax
from jax.experimental import pallas as pl
from jax.experimental.pallas import tpu as pltpu

F32 = jnp.float32
BF16 = jnp.bfloat16

EPS = 1e-6
HEAD_DIM = 128
MXU_N = 256
GRID_W = 64
NA_KH = 8
NA_KW = 16
NA_ROWS_PER_BLOCK = 8
NA_HALO_ROWS = 4
NA_HEADS_PER_STEP = 4
MASK_VALUE = -1e30
NA_INVALID_ROW = 2 * NA_KH - 1
LOG2E = 1.4426950408889634

DFT_R1 = 64
DFT_SUB = 8
DFT_A_CHUNKS = 2

MIB = 1024 * 1024
VMEM_LIMIT = 48 * MIB


def _cparams(sem, vmem=VMEM_LIMIT):
    return pltpu.CompilerParams(dimension_semantics=sem, vmem_limit_bytes=vmem)


def _rms(x, eps=EPS):
    return x * lax.rsqrt(jnp.mean(x * x, axis=-1, keepdims=True) + eps)


def _head_norm_store(dst_ref, acc, gain, n_heads, col0=0):
    for h in range(n_heads):
        sl = slice(col0 + h * HEAD_DIM, col0 + (h + 1) * HEAD_DIM)
        dl = slice(h * HEAD_DIM, (h + 1) * HEAD_DIM)
        dst_ref[:, dl] = (_rms(acc[:, sl]) * gain[:, sl]).astype(dst_ref.dtype)


def _inproj_kernel(x_ref, g_ref, w_ref, hg_ref, p_ref, u_ref, *, normed_heads, u_col, row_chunks):
    rows = x_ref.shape[0] // row_chunks
    u_width = u_ref.shape[1]
    heads_per_chunk = MXU_N // HEAD_DIM
    chunks = sorted(range(w_ref.shape[1] // MXU_N),
                    key=lambda c: not any(c * heads_per_chunk + hh in normed_heads
                                          for hh in range(heads_per_chunk)))
    for r in range(row_chunks):
        rs = slice(r * rows, (r + 1) * rows)
        h = (_rms(x_ref[rs, :]) * g_ref[...]).astype(BF16)
        for c in chunks:
            acc = jnp.dot(h, w_ref[:, c * MXU_N:(c + 1) * MXU_N], preferred_element_type=F32)
            for hh in range(heads_per_chunk):
                blk = acc[:, hh * HEAD_DIM:(hh + 1) * HEAD_DIM]
                head = c * heads_per_chunk + hh
                sl = slice(head * HEAD_DIM, (head + 1) * HEAD_DIM)
                if head in normed_heads:
                    p_ref[rs, sl] = (_rms(blk) * hg_ref[:, sl]).astype(p_ref.dtype)
                else:
                    p_ref[rs, sl] = blk.astype(p_ref.dtype)
                if u_col <= sl.start < u_col + u_width:
                    u_ref[rs, sl.start - u_col:sl.stop - u_col] = blk


def _in_projection(x2, gain, w_in, head_gain, normed_heads, layer, u_col, u_width, tm=512, row_chunks=2):
    n, d = x2.shape
    d_in = w_in.shape[2]
    return pl.pallas_call(
        functools.partial(_inproj_kernel, normed_heads=normed_heads, u_col=u_col, row_chunks=row_chunks),
        grid=(n // tm,),
        in_specs=[
            pl.BlockSpec((tm, d), lambda i: (i, 0)),
            pl.BlockSpec((1, d), lambda i: (0, 0)),
            pl.BlockSpec((None, d, d_in), lambda i: (layer, 0, 0)),
            pl.BlockSpec((1, d_in), lambda i: (0, 0)),
        ],
        out_specs=[
            pl.BlockSpec((tm, d_in), lambda i: (i, 0)),
            pl.BlockSpec((tm, u_width), lambda i: (i, 0)),
        ],
        out_shape=[
            jax.ShapeDtypeStruct((n, d_in), BF16),
            jax.ShapeDtypeStruct((n, u_width), F32),
        ],
        compiler_params=_cparams(("parallel",)),
        name="in_projection",
    )(x2, gain, w_in, head_gain)


def _na_build_bias(variant, e_ref, bias_scr, hh):
    rq, halo = NA_ROWS_PER_BLOCK, NA_HALO_ROWS
    half_rows = rq // 2
    n_pairs = (half_rows + NA_KH) // 2
    lane = lax.broadcasted_iota(jnp.int32, (GRID_W, 2 * GRID_W), 1)
    for il in range(rq):
        jl0 = jnp.where(variant == 0, max(il, halo), jnp.where(variant == 2, min(il, halo), il))
        half = il // half_rows
        rows = slice((il % half_rows) * GRID_W, (il % half_rows + 1) * GRID_W)
        for pair in range(n_pairs):
            blocks = []
            for jl in (half * halo + 2 * pair, half * halo + 2 * pair + 1):
                inside = jnp.logical_and(jl >= jl0, jl < jl0 + NA_KH)
                idx = jnp.where(inside, jl - il - halo + NA_KH - 1, NA_INVALID_ROW)
                blocks.append(e_ref[hh, idx])
            bias_scr[hh, half, rows, pair * 2 * GRID_W:(pair + 1) * 2 * GRID_W] = (
                jnp.where(lane < GRID_W, blocks[0], blocks[1]))


def _na_kernel(q_ref, kp_ref, kc_ref, kn_ref, vp_ref, vc_ref, vn_ref, e_ref, g_ref, o_ref,
               bias_scr, vx_scr):
    n_batch = q_ref.shape[0]
    n_heads = q_ref.shape[2] // HEAD_DIM
    rb = pl.program_id(1)
    last = pl.num_programs(1) - 1
    half_tokens = q_ref.shape[1] // 2
    halo_tokens = kp_ref.shape[1]
    win_tokens = half_tokens + 2 * halo_tokens
    contract_last = (((1,), (1,)), ((), ()))

    @pl.when(jnp.logical_or(rb <= 1, rb == last))
    def _():
        variant = jnp.where(rb == 0, 0, jnp.where(rb == last, 2, 1))
        for hh in range(n_heads):
            _na_build_bias(variant, e_ref, bias_scr, hh)
        vx_scr[:, :, :, HEAD_DIM:] = jnp.ones(vx_scr.shape[:3] + (HEAD_DIM,), BF16)

    k_segments = ((kp_ref, kc_ref), (kc_ref, kn_ref))
    for hh in range(n_heads):
        hs = slice(hh * HEAD_DIM, (hh + 1) * HEAD_DIM)
        for b in range(n_batch):
            row0 = 0
            for v_ref in (vp_ref, vc_ref, vn_ref):
                vx_scr[b, hh, row0:row0 + v_ref.shape[1], :HEAD_DIM] = v_ref[b, :, hs]
                row0 += v_ref.shape[1]
            for half in range(2):
                rows = slice(half * half_tokens, (half + 1) * half_tokens)
                keys = slice(half * halo_tokens, half * halo_tokens + win_tokens)
                q = q_ref[b, rows, hs]
                s = jnp.concatenate(
                    [lax.dot_general(q, k_ref[b, :, hs], contract_last, preferred_element_type=F32)
                     for k_ref in k_segments[half]], axis=-1)
                s = s + bias_scr[hh, half]
                m = jnp.max(s, axis=-1, keepdims=True)
                pb = jnp.exp2((s - m).astype(BF16))
                ol = jnp.dot(pb, vx_scr[b, hh, keys, :], preferred_element_type=F32)
                o = ol[:, :HEAD_DIM] * (1.0 / ol[:, HEAD_DIM:])
                o_ref[b, rows, hs] = (_rms(o) * g_ref[:, hs]).astype(o_ref.dtype)


def _na_bias_blocks(rpb):
    n_heads = rpb.shape[0]
    kw = min(NA_KW, GRID_W)
    col = np.arange(GRID_W)
    col_start = np.clip(col - kw // 2, 0, GRID_W - kw)
    col_ok = (col[None, :] >= col_start[:, None]) & (col[None, :] < col_start[:, None] + kw)
    col_idx = np.clip(col[None, :] - col[:, None] + NA_KW - 1, 0, 2 * NA_KW - 2)
    onehot = jnp.asarray(col_idx[None] == np.arange(2 * NA_KW - 1)[:, None, None], F32)
    picked = jnp.einsum("hde,eqk->hdqk", rpb, onehot, precision=lax.Precision.HIGHEST)
    toe = jnp.where(col_ok[None, None], picked * LOG2E, MASK_VALUE)
    toe = jnp.concatenate([toe, jnp.full((n_heads, 1, GRID_W, GRID_W), MASK_VALUE, F32)], axis=1)
    return jnp.concatenate([toe, toe], axis=-1)


def _neighbourhood_attention(proj, bias_blocks, gain, n_heads):
    b, s, _ = proj.shape
    width = n_heads * HEAD_DIM
    tq = NA_ROWS_PER_BLOCK * GRID_W
    th = NA_HALO_ROWS * GRID_W
    n_blocks = s // tq
    n_halo_blocks = s // th
    per = tq // th
    assert n_blocks >= 3 and NA_KH == 2 * NA_HALO_ROWS == NA_ROWS_PER_BLOCK

    hps = NA_HEADS_PER_STEP
    wb = hps * HEAD_DIM
    assert n_heads % hps == 0

    def cur(col0):
        return pl.BlockSpec((b, tq, wb), lambda h, rb: (0, rb, col0 + h))

    def prev(col0):
        return pl.BlockSpec((b, th, wb), lambda h, rb: (0, jnp.maximum(per * rb - 1, 0), col0 + h))

    def nxt(col0):
        return pl.BlockSpec((b, th, wb),
                            lambda h, rb: (0, jnp.minimum(per * rb + per, n_halo_blocks - 1), col0 + h))

    k0, v0 = n_heads // hps, 2 * n_heads // hps
    return pl.pallas_call(
        _na_kernel,
        grid=(n_heads // hps, n_blocks),
        in_specs=[
            cur(0), prev(k0), cur(k0), nxt(k0), prev(v0), cur(v0), nxt(v0),
            pl.BlockSpec((hps,) + bias_blocks.shape[1:], lambda h, rb: (h, 0, 0, 0)),
            pl.BlockSpec((1, wb), lambda h, rb: (0, h)),
        ],
        out_specs=cur(0),
        out_shape=jax.ShapeDtypeStruct((b, s, width), BF16),
        scratch_shapes=[pltpu.VMEM((hps, 2, tq // 2, tq // 2 + 2 * th), F32),
                        pltpu.VMEM((b, hps, tq + 2 * th, 2 * HEAD_DIM), BF16)],
        compiler_params=_cparams(("arbitrary", "arbitrary"), vmem=52 * MIB),
        name="neighbourhood_attention",
    )(proj, proj, proj, proj, proj, proj, proj, bias_blocks, gain)


def _mem_kv_kernel(m_ref, g_ref, w_ref, kg_ref, k_ref, v_ref, h_scr):
    j = pl.program_id(0)

    @pl.when(j == 0)
    def _():
        h_scr[...] = (_rms(m_ref[...]) * g_ref[...]).astype(BF16)

    acc = jnp.dot(h_scr[...], w_ref[...], preferred_element_type=F32)

    @pl.when(j == 0)
    def _():
        _head_norm_store(k_ref, acc, kg_ref[...], acc.shape[1] // HEAD_DIM)

    @pl.when(j == 1)
    def _():
        v_ref[...] = acc.astype(v_ref.dtype)


def _memory_kv(mem2, gain, w_kv, k_gain, layer):
    n, d = mem2.shape
    width = w_kv.shape[2] // 2
    full = lambda j: (0, 0)
    return pl.pallas_call(
        _mem_kv_kernel,
        grid=(2,),
        in_specs=[
            pl.BlockSpec((n, d), full),
            pl.BlockSpec((1, d), full),
            pl.BlockSpec((None, d, width), lambda j: (layer, 0, j)),
            pl.BlockSpec((1, width), full),
        ],
        out_specs=[pl.BlockSpec((n, width), full), pl.BlockSpec((n, width), full)],
        out_shape=[jax.ShapeDtypeStruct((n, width), BF16), jax.ShapeDtypeStruct((n, width), BF16)],
        scratch_shapes=[pltpu.VMEM((n, d), BF16)],
        compiler_params=_cparams(("arbitrary",)),
        name="memory_kv",
    )(mem2, gain, w_kv, k_gain)


def _mem_attn_kernel(q_ref, m_ref, mg_ref, w_ref, kg_ref, g_ref, o_ref, k_scr, vx_scr):
    n_heads = q_ref.shape[2] // HEAD_DIM
    n_mem = m_ref.shape[1]
    width = n_heads * HEAD_DIM

    @pl.when(pl.program_id(1) == 0)
    def _():
        hm = (_rms(m_ref[0]) * mg_ref[...]).astype(BF16)
        kv = jnp.dot(hm, w_ref[...], preferred_element_type=F32)
        for h in range(n_heads):
            sl = slice(h * HEAD_DIM, (h + 1) * HEAD_DIM)
            k_scr[:, sl] = (_rms(kv[:, sl]) * kg_ref[:, sl]).astype(BF16)
            vx_scr[h, :, :HEAD_DIM] = kv[:, width + h * HEAD_DIM:width + (h + 1) * HEAD_DIM].astype(BF16)
            vx_scr[h, :, HEAD_DIM:] = jnp.ones((n_mem, HEAD_DIM), BF16)

    for h in range(n_heads):
        sl = slice(h * HEAD_DIM, (h + 1) * HEAD_DIM)
        s = lax.dot_general(q_ref[0, :, sl], k_scr[:, sl], (((1,), (1,)), ((), ())),
                            preferred_element_type=F32)
        m = jnp.max(s, axis=-1, keepdims=True)
        pb = jnp.exp2((s - m).astype(BF16))
        ol = jnp.dot(pb, vx_scr[h], preferred_element_type=F32)
        o = ol[:, :HEAD_DIM] * (1.0 / ol[:, HEAD_DIM:])
        o_ref[0, :, sl] = (_rms(o) * g_ref[:, sl]).astype(o_ref.dtype)


def _memory_attention(proj, q_col, mem, mem_gain, w_kv, k_gain, gain, layer, tm=2048):
    b, s, _ = proj.shape
    m, d = mem.shape[1], mem.shape[2]
    width = w_kv.shape[2] // 2
    assert q_col % width == 0
    return pl.pallas_call(
        _mem_attn_kernel,
        grid=(b, s // tm),
        in_specs=[
            pl.BlockSpec((1, tm, width), lambda bi, t: (bi, t, q_col // width)),
            pl.BlockSpec((1, m, d), lambda bi, t: (bi, 0, 0)),
            pl.BlockSpec((1, d), lambda bi, t: (0, 0)),
            pl.BlockSpec((None, d, 2 * width), lambda bi, t: (layer, 0, 0)),
            pl.BlockSpec((1, width), lambda bi, t: (0, 0)),
            pl.BlockSpec((1, width), lambda bi, t: (0, 0)),
        ],
        out_specs=pl.BlockSpec((1, tm, width), lambda bi, t: (bi, t, 0)),
        out_shape=jax.ShapeDtypeStruct((b, s, width), BF16),
        scratch_shapes=[pltpu.VMEM((m, width), BF16),
                        pltpu.VMEM((width // HEAD_DIM, m, 2 * HEAD_DIM), BF16)],
        compiler_params=_cparams(("arbitrary", "arbitrary")),
        name="memory_attention",
    )(proj, mem, mem_gain, w_kv, k_gain, gain)


def _dft_matrices(seq):
    r1 = DFT_R1
    r2 = seq // r1
    sub = DFT_SUB
    k2 = np.arange(r2, dtype=np.int64)
    s2 = np.arange(r2, dtype=np.int64)
    s1 = np.arange(r1, dtype=np.int64)
    k1 = np.arange(r1, dtype=np.int64)
    n_ch = DFT_A_CHUNKS
    th = 2.0 * np.pi * ((k2[:, None] * s2[None, :]) % r2).astype(np.float64) / r2
    fa = np.stack([np.cos(th), -np.sin(th)], axis=0)
    ma = np.einsum("cks,ab->ckasb", fa, np.eye(sub))
    ma = ma.reshape(2, n_ch, r2 // n_ch, sub, r2 * sub).transpose(1, 0, 3, 2, 4)
    ma = ma.reshape(n_ch, 2 * r2 * sub // n_ch, r2 * sub)
    tw = 2.0 * np.pi * ((s1[:, None] * k2[None, :]) % seq).astype(np.float64) / seq
    tw = tw.reshape(r1 // sub, sub, n_ch, r2 // n_ch).transpose(0, 2, 1, 3)
    tw = tw.reshape(r1 // sub, n_ch, r2 * sub // n_ch, 1)
    tw_shape = tw.shape[:3] + (HEAD_DIM,)
    tw_cos = jnp.broadcast_to(jnp.asarray(np.cos(tw), F32), tw_shape)
    tw_sin = jnp.broadcast_to(jnp.asarray(np.sin(tw), F32), tw_shape)
    ph = 2.0 * np.pi * ((k1[:, None] * s1[None, :]) % r1).astype(np.float64) / r1
    g = np.stack([np.cos(ph), np.sin(ph)], 0)
    mc = np.einsum("cks,ab->kacsb", g, np.eye(sub))
    mc = mc.reshape(r1 * sub, 2 * sub * r1)
    cc = np.arange(HEAD_DIM, dtype=np.int64)
    pc = 2.0 * np.pi * ((cc[:, None] * cc[None, :]) % HEAD_DIM).astype(np.float64) / HEAD_DIM
    mch = np.concatenate([np.cos(pc), -np.sin(pc)], axis=0) / np.sqrt(float(seq) * HEAD_DIM)
    ma, mc, mch = (jnp.asarray(m, F32).astype(BF16) for m in (ma, mc, mch))
    return ma, tw_cos, tw_sin, mc, mch


def _fft_a_kernel(u_ref, ma_ref, twc_ref, tws_ref, mch_ref, wf_ref, t_ref):
    r2, width = u_ref.shape[1], u_ref.shape[3]
    sub = DFT_SUB
    n_chunks = ma_ref.shape[0]
    k2c = r2 // n_chunks
    half = k2c * sub
    blocks = []
    for g in range(width // HEAD_DIM):
        gg = jnp.dot(mch_ref[...], wf_ref[g], preferred_element_type=F32)
        g_re, g_im = gg[:HEAD_DIM], gg[HEAD_DIM:]
        blocks.append(jnp.concatenate([jnp.concatenate([g_re, g_im], axis=1),
                                       jnp.concatenate([-g_im, g_re], axis=1)], axis=0).astype(BF16))
    for j in range(u_ref.shape[2] // sub):
        ss = slice(j * sub, (j + 1) * sub)
        xb = u_ref[0, :, ss, :].reshape(r2 * sub, width).astype(BF16)
        for ch in range(n_chunks):
            res = jnp.dot(ma_ref[ch], xb, preferred_element_type=F32)
            tw_cos, tw_sin = twc_ref[j, ch], tws_ref[j, ch]
            for g in range(width // HEAD_DIM):
                sl = slice(g * HEAD_DIM, (g + 1) * HEAD_DIM)
                re, im = res[:half, sl], res[half:, sl]
                t_re = re * tw_cos + im * tw_sin
                t_im = im * tw_cos - re * tw_sin
                t2 = jnp.dot(jnp.concatenate([t_re, t_im], axis=1).astype(BF16), blocks[g],
                             preferred_element_type=F32)
                ks = slice(ch * k2c, (ch + 1) * k2c)
                for c in range(2):
                    part = t2[:, c * HEAD_DIM:(c + 1) * HEAD_DIM]
                    t_ref[0, c, ss, ks, sl] = part.reshape(sub, k2c, HEAD_DIM).astype(t_ref.dtype)


def _fft_stage_a(u4, ma, tw_cos, tw_sin, mch, w_f, layer, s1_tiles=2):
    b, r2, r1, width = u4.shape
    sub = DFT_SUB * s1_tiles
    groups = width // HEAD_DIM
    tw_spec = pl.BlockSpec((s1_tiles,) + tw_cos.shape[1:], lambda t, bi: (t, 0, 0, 0))
    return pl.pallas_call(
        _fft_a_kernel,
        grid=(r1 // sub, b),
        in_specs=[
            pl.BlockSpec((1, r2, sub, width), lambda t, bi: (bi, 0, t, 0)),
            pl.BlockSpec(ma.shape, lambda t, bi: (0, 0, 0)),
            tw_spec, tw_spec,
            pl.BlockSpec(mch.shape, lambda t, bi: (0, 0)),
            pl.BlockSpec((None, groups, HEAD_DIM, HEAD_DIM), lambda t, bi: (layer, 0, 0, 0)),
        ],
        out_specs=pl.BlockSpec((1, 2, sub, r2, width), lambda t, bi: (bi, 0, t, 0, 0)),
        out_shape=jax.ShapeDtypeStruct((b, 2, r1, r2, width), BF16),
        compiler_params=_cparams(("arbitrary", "arbitrary")),
        name="fourier_stage_a",
    )(u4, ma, tw_cos, tw_sin, mch, w_f)


def _fft_c_kernel(t_ref, mc_ref, g_ref, o_ref):
    sub = DFT_SUB
    r1, width = t_ref.shape[2], t_ref.shape[4]
    t32 = [t_ref[0, c].astype(F32) for c in range(2)]
    for j in range(t_ref.shape[3] // sub):
        ks = slice(j * sub, (j + 1) * sub)
        tb = jnp.concatenate([t[:, ks, :].reshape(r1 * sub, width) for t in t32],
                             axis=0).astype(BF16)
        res = jnp.dot(mc_ref[...], tb, preferred_element_type=F32)
        for g in range(width // HEAD_DIM):
            sl = slice(g * HEAD_DIM, (g + 1) * HEAD_DIM)
            z = _rms(res[:, sl]) * g_ref[:, sl]
            o_ref[0, :, ks, sl] = z.reshape(r1, sub, HEAD_DIM)


def _fft_stage_c(t5, mc, gain, k2_tiles=4):
    b, _, r1, r2, width = t5.shape
    sub = DFT_SUB * k2_tiles
    return pl.pallas_call(
        _fft_c_kernel,
        grid=(r2 // sub, b),
        in_specs=[
            pl.BlockSpec((1, 2, r1, sub, width), lambda t, bi: (bi, 0, 0, t, 0)),
            pl.BlockSpec(mc.shape, lambda t, bi: (0, 0)),
            pl.BlockSpec((1, width), lambda t, bi: (0, 0)),
        ],
        out_specs=pl.BlockSpec((1, r1, sub, width), lambda t, bi: (bi, 0, t, 0)),
        out_shape=jax.ShapeDtypeStruct((b, r1, r2, width), F32),
        compiler_params=_cparams(("arbitrary", "arbitrary")),
        name="fourier_stage_c",
    )(t5, mc, gain)


def _outproj_kernel(x_ref, ya_ref, yf_ref, ym_ref, w_ref, o_ref):
    na = ya_ref.shape[1]
    nf = yf_ref.shape[1]
    acc = x_ref[...]
    acc = acc + jnp.dot(ya_ref[...], w_ref[:na, :], preferred_element_type=F32)
    acc = acc + jnp.dot(yf_ref[...].astype(BF16), w_ref[na:na + nf, :], preferred_element_type=F32)
    acc = acc + jnp.dot(ym_ref[...], w_ref[na + nf:, :], preferred_element_type=F32)
    o_ref[...] = acc


def _out_projection(x2, y_na, y_f, y_m, w_out, layer, tm=512):
    n, d = x2.shape
    row = lambda i: (i, 0)
    return pl.pallas_call(
        _outproj_kernel,
        grid=(n // tm,),
        in_specs=[
            pl.BlockSpec((tm, d), row),
            pl.BlockSpec((tm, y_na.shape[1]), row),
            pl.BlockSpec((tm, y_f.shape[1]), row),
            pl.BlockSpec((tm, y_m.shape[1]), row),
            pl.BlockSpec((None,) + w_out.shape[1:], lambda i: (layer, 0, 0)),
        ],
        out_specs=pl.BlockSpec((tm, d), row),
        out_shape=jax.ShapeDtypeStruct((n, d), F32),
        compiler_params=_cparams(("parallel",)),
        name="out_projection",
    )(x2, y_na, y_f, y_m, w_out)


def _ffn_kernel(x_ref, g_ref, wg_ref, wu_ref, wd_ref, o_ref, h_scr):
    f = pl.program_id(1)

    def delta():
        h = h_scr[...]
        gate = jnp.dot(h, wg_ref[...], preferred_element_type=F32)
        up = jnp.dot(h, wu_ref[...], preferred_element_type=F32)
        act = (gate * jax.nn.sigmoid(gate) * up).astype(BF16)
        return jnp.dot(act, wd_ref[...], preferred_element_type=F32)

    @pl.when(f == 0)
    def _():
        x = x_ref[...]
        h_scr[...] = (_rms(x) * g_ref[...]).astype(BF16)
        o_ref[...] = x_ref[...] + delta()

    @pl.when(f != 0)
    def _():
        o_ref[...] += delta()


def _ffn(x2, gain, w_gate, w_up, w_down, layer, tm=1024, tf=512):
    n, d = x2.shape
    d_ff = w_gate.shape[2]
    row = lambda i, f: (i, 0)
    return pl.pallas_call(
        _ffn_kernel,
        grid=(n // tm, d_ff // tf),
        in_specs=[
            pl.BlockSpec((tm, d), row),
            pl.BlockSpec((1, d), lambda i, f: (0, 0)),
            pl.BlockSpec((None, d, tf), lambda i, f: (layer, 0, f)),
            pl.BlockSpec((None, d, tf), lambda i, f: (layer, 0, f)),
            pl.BlockSpec((None, tf, d), lambda i, f: (layer, f, 0)),
        ],
        out_specs=pl.BlockSpec((tm, d), row),
        out_shape=jax.ShapeDtypeStruct((n, d), F32),
        scratch_shapes=[pltpu.VMEM((tm, d), BF16)],
        compiler_params=_cparams(("parallel", "arbitrary"), vmem=56 * MIB),
        name="swiglu_ffn",
    )(x2, gain, w_gate, w_up, w_down)


def kernel(x, mem, attn_norm, w_in, na_q_norm, na_k_norm, na_rpb, w_fourier, mem_norm, w_mem_kv,
           mem_q_norm, mem_k_norm, out_norm, w_out, ffn_norm, w_gate, w_up, w_down):
    b, s, d = x.shape
    m = mem.shape[1]
    depth = w_in.shape[0]
    n_mem_heads = w_mem_kv.shape[2] // 2 // HEAD_DIM
    fourier_width = w_fourier.shape[1] * HEAD_DIM
    mem_width = n_mem_heads * HEAD_DIM
    na_width = (w_in.shape[2] - fourier_width - mem_width) // 3
    na_heads = na_width // HEAD_DIM
    q_scale = HEAD_DIM ** -0.5 * LOG2E

    ma, tw_cos, tw_sin, mc, mch = _dft_matrices(s)
    x2 = x.reshape(b * s, d)
    mem2 = mem.reshape(b * m, d)
    w_in_b, w_kv_b, w_f_b, w_out_b = (w.astype(BF16) for w in (w_in, w_mem_kv, w_fourier, w_out))
    w_gate_b, w_up_b, w_down_b = (w.astype(BF16) for w in (w_gate, w_up, w_down))
    first_mem_head = (3 * na_width + fourier_width) // HEAD_DIM
    normed_heads = frozenset(range(2 * na_heads)) | frozenset(
        range(first_mem_head, first_mem_head + n_mem_heads))

    for l in range(depth):
        head_gain = jnp.concatenate([
            jnp.tile(na_q_norm[l] * q_scale, na_heads),
            jnp.tile(na_k_norm[l], na_heads),
            jnp.ones((na_width + fourier_width,), F32),
            jnp.tile(mem_q_norm[l] * q_scale, n_mem_heads),
        ]).reshape(1, -1)
        proj, u_f = _in_projection(x2, attn_norm[l].reshape(1, d), w_in_b, head_gain, normed_heads, l,
                                   u_col=3 * na_width, u_width=fourier_width)
        proj = proj.reshape(b, s, -1)

        gain_out = out_norm[l].reshape(1, -1)
        y_na = _neighbourhood_attention(proj, _na_bias_blocks(na_rpb[l]), gain_out[:, :na_width], na_heads)

        y_m = _memory_attention(proj, 3 * na_width + fourier_width, mem, mem_norm[l].reshape(1, d), w_kv_b,
                                jnp.tile(mem_k_norm[l], n_mem_heads).reshape(1, -1),
                                gain_out[:, na_width + fourier_width:], l)

        t5 = _fft_stage_a(u_f.reshape(b, s // DFT_R1, DFT_R1, fourier_width), ma, tw_cos, tw_sin,
                          mch, w_f_b, l)
        y_f = _fft_stage_c(t5, mc, gain_out[:, na_width:na_width + fourier_width])

        x2 = _out_projection(x2, y_na.reshape(b * s, na_width), y_f.reshape(b * s, fourier_width),
                             y_m.reshape(b * s, mem_width), w_out_b, l)
        x2 = _ffn(x2, ffn_norm[l].reshape(1, d), w_gate_b, w_up_b, w_down_b, l)
    return x2.reshape(b, s, d)
```
